```python
import jax, jax.numpy as jnp
from jax import lax
import numpy as np

D_MODEL = 1024
BATCH = 2
SEQ = 8192
DEPTH = 4

GRID_W = 64
CTX_LEN = 256
HEAD_DIM = 64
NA_HEADS = 8
NA_WIN_H = 8
NA_WIN_W = 16
GQA_Q_HEADS = 8
GQA_KV_HEADS = 2
GQA_REP = GQA_Q_HEADS // GQA_KV_HEADS
NA_WIDTH = NA_HEADS * HEAD_DIM
GQA_Q_WIDTH = GQA_Q_HEADS * HEAD_DIM
GQA_KV_WIDTH = GQA_KV_HEADS * HEAD_DIM
IN_SIZES = (NA_WIDTH, NA_WIDTH, NA_WIDTH, GQA_Q_WIDTH, GQA_KV_WIDTH, GQA_KV_WIDTH, D_MODEL, D_MODEL)
IN_COLS = sum(IN_SIZES)
IN_SPLITS = tuple(int(v) for v in np.cumsum(IN_SIZES)[:-1])
D_FF = -(-(8 * D_MODEL) // (3 * 256)) * 256
Q_BLOCK = 128
ROPE_THETA = 10000.0
EPS = 1e-6
SCALE = HEAD_DIM ** -0.5

kernel_name = "hybrid_natten_gqa_prefix_dit"


def rms_norm(x, g):
    xf = x.astype(jnp.float32)
    y = xf * lax.rsqrt(jnp.mean(xf * xf, axis=-1, keepdims=True) + EPS)
    return (y * g.astype(jnp.float32)).astype(x.dtype)


def modulate(h, shift, scale):
    return h * (1 + scale) + shift


def heads(t, n_heads):
    b, n, _ = t.shape
    return t.reshape(b, n, n_heads, HEAD_DIM).transpose(0, 2, 1, 3)


def merge_heads(t):
    b, h, n, d = t.shape
    return t.transpose(0, 2, 1, 3).reshape(b, n, h * d)


def axial_rope_tables(n_tokens):
    t = jnp.arange(n_tokens)
    row = (t // GRID_W).astype(jnp.float32)
    col = (t % GRID_W).astype(jnp.float32)
    half = HEAD_DIM // 2
    inv = ROPE_THETA ** (-jnp.arange(0, half, 2, dtype=jnp.float32) / half)
    ang = jnp.concatenate([row[:, None] * inv, col[:, None] * inv], axis=-1)
    return jnp.cos(ang), jnp.sin(ang)


def apply_rope(x, cos, sin):
    xf = x.astype(jnp.float32).reshape(x.shape[:-1] + (HEAD_DIM // 2, 2))
    x0, x1 = xf[..., 0], xf[..., 1]
    out = jnp.stack([x0 * cos - x1 * sin, x0 * sin + x1 * cos], axis=-1)
    return out.reshape(x.shape).astype(x.dtype)


def na_window_indices(rows):
    kh = min(NA_WIN_H, rows)
    r = jnp.arange(rows)
    col = jnp.arange(GRID_W)
    r_start = jnp.clip(r - kh // 2, 0, rows - kh)
    row_idx = r_start[:, None] + jnp.arange(kh)[None, :]
    c_start = jnp.clip(col - NA_WIN_W // 2, 0, GRID_W - NA_WIN_W)
    in_win = (col[None, :] >= c_start[:, None]) & (col[None, :] < c_start[:, None] + NA_WIN_W)
    dr_idx = row_idx - r[:, None] + (NA_WIN_H - 1)
    dc = col[None, :] - col[:, None]
    dc_idx = jnp.clip(dc, -(NA_WIN_W - 1), NA_WIN_W - 1) + (NA_WIN_W - 1)
    return row_idx, in_win, dr_idx, dc_idx


def na_bias(rpb, in_win, dr_idx, dc_idx):
    b = rpb.astype(jnp.float32)[:, dr_idx[:, None, :, None], dc_idx[None, :, None, :]]
    return jnp.where(in_win[None, None, :, None, :], b, -jnp.inf)


def na_latent(q, k, v, k_ctx, v_ctx, bias, row_idx):
    b, h, s, d = q.shape
    rows = s // GRID_W
    kh = row_idx.shape[1]
    qg = q.reshape(b, h, rows, GRID_W, d)
    kg = k.reshape(b, h, rows, GRID_W, d)[:, :, row_idx]
    vg = v.reshape(b, h, rows, GRID_W, d)[:, :, row_idx]
    s_win = jnp.einsum('bhrwd,bhrjud->bhrwju', qg, kg).astype(jnp.float32) * SCALE + bias[None]
    s_win = s_win.reshape(b, h, rows, GRID_W, kh * GRID_W)
    s_ctx = jnp.einsum('bhrwd,bhld->bhrwl', qg, k_ctx).astype(jnp.float32) * SCALE
    p = jax.nn.softmax(jnp.concatenate([s_win, s_ctx], axis=-1), axis=-1).astype(v.dtype)
    p_win = p[..., :kh * GRID_W].reshape(b, h, rows, GRID_W, kh, GRID_W)
    p_ctx = p[..., kh * GRID_W:]
    o = jnp.einsum('bhrwju,bhrjud->bhrwd', p_win, vg) + jnp.einsum('bhrwl,bhld->bhrwd', p_ctx, v_ctx)
    return o.reshape(b, h, s, d)


def softmax_attn(q, k, v):
    s = jnp.einsum('bgrqd,bgkd->bgrqk', q, k).astype(jnp.float32) * SCALE
    p = jax.nn.softmax(s, axis=-1).astype(v.dtype)
    return jnp.einsum('bgrqk,bgkd->bgrqd', p, v)


def gqa_latent(q, k_all, v_all):
    b, g, r, s, d = q.shape
    nb = s // Q_BLOCK
    qb = q.reshape(b, g, r, nb, Q_BLOCK, d).transpose(3, 0, 1, 2, 4, 5)
    o = lax.map(lambda qblk: softmax_attn(qblk, k_all, v_all), qb)
    return o.transpose(1, 2, 3, 0, 4, 5).reshape(b, g, r, s, d)


def branch_merge(ya, yb, ga, gb, w_pa, w_pb, w_o):
    merged = jax.nn.sigmoid(ga) * (ya @ w_pa) + jax.nn.sigmoid(gb) * (yb @ w_pb)
    return merged @ w_o


def swiglu(h, w_in, w_out):
    a, u = jnp.split(h @ w_in, 2, axis=-1)
    return (jax.nn.silu(a) * u) @ w_out


def setup_inputs(seed: int = 0) -> dict:
    key = jax.random.key(seed)
    ks = jax.random.split(key, 18)
    f32 = jnp.float32
    nrm = lambda k, shp, sc: jax.random.normal(k, shp, f32) * sc
    return {
        "x": nrm(ks[0], (BATCH, SEQ, D_MODEL), 1.0),
        "c": nrm(ks[1], (BATCH, D_MODEL), 1.0),
        "ctx": nrm(ks[2], (BATCH, CTX_LEN, D_MODEL), 1.0),
        "c_ctx": nrm(ks[3], (D_MODEL,), 1.0),
        "w_mod": nrm(ks[4], (DEPTH, D_MODEL, 6 * D_MODEL), D_MODEL ** -0.5),
        "b_mod": nrm(ks[5], (DEPTH, 6 * D_MODEL), 0.01),
        "norm1": 1.0 + nrm(ks[6], (DEPTH, D_MODEL), 0.02),
        "w_in": nrm(ks[7], (DEPTH, D_MODEL, IN_COLS), D_MODEL ** -0.5),
        "na_rpb": nrm(ks[8], (DEPTH, NA_HEADS, 2 * NA_WIN_H - 1, 2 * NA_WIN_W - 1), 0.1),
        "q_gain": 1.0 + nrm(ks[9], (DEPTH, HEAD_DIM), 0.02),
        "k_gain": 1.0 + nrm(ks[10], (DEPTH, HEAD_DIM), 0.02),
        "w_pa": nrm(ks[11], (DEPTH, NA_WIDTH, D_MODEL), NA_WIDTH ** -0.5),
        "w_pb": nrm(ks[12], (DEPTH, GQA_Q_WIDTH, D_MODEL), GQA_Q_WIDTH ** -0.5),
        "w_o": nrm(ks[13], (DEPTH, D_MODEL, D_MODEL), D_MODEL ** -0.5),
        "norm2": 1.0 + nrm(ks[14], (DEPTH, D_MODEL), 0.02),
        "w_ffn_in": nrm(ks[15], (DEPTH, D_MODEL, 2 * D_FF), D_MODEL ** -0.5),
        "w_ffn_out": nrm(ks[16], (DEPTH, D_FF, D_MODEL), D_FF ** -0.5),
        "final_norm": 1.0 + nrm(ks[17], (D_MODEL,), 0.02),
    }


def reference(x, c, ctx, c_ctx, w_mod, b_mod, norm1, w_in, na_rpb, q_gain, k_gain, w_pa, w_pb, w_o,
              norm2, w_ffn_in, w_ffn_out, final_norm):
    b, s, _ = x.shape
    n_ctx = ctx.shape[1]
    rows = s // GRID_W
    cos, sin = axial_rope_tables(s)
    row_idx, in_win, dr_idx, dc_idx = na_window_indices(rows)
    silu_c = jax.nn.silu(c)
    silu_cc = jax.nn.silu(c_ctx)

    for l in range(DEPTH):
        last = l == DEPTH - 1
        mod = silu_c @ w_mod[l] + b_mod[l]
        mod_c = silu_cc @ w_mod[l] + b_mod[l]
        sh1, sc1, g1, sh2, sc2, g2 = jnp.split(mod[:, None, :], 6, axis=-1)
        csh1, csc1, cg1, csh2, csc2, cg2 = jnp.split(mod_c, 6)

        h = modulate(rms_norm(x, norm1[l]), sh1, sc1)
        hc = modulate(rms_norm(ctx, norm1[l]), csh1, csc1)
        na_q, na_k, na_v, gq, gk, gv, ga, gb = jnp.split(h @ w_in[l], IN_SPLITS, axis=-1)
        na_qc, na_kc, na_vc, gqc, gkc, gvc, gac, gbc = jnp.split(hc @ w_in[l], IN_SPLITS, axis=-1)

        qa, ka, va = heads(na_q, NA_HEADS), heads(na_k, NA_HEADS), heads(na_v, NA_HEADS)
        qa_c, ka_c, va_c = heads(na_qc, NA_HEADS), heads(na_kc, NA_HEADS), heads(na_vc, NA_HEADS)
        bias = na_bias(na_rpb[l], in_win, dr_idx, dc_idx)
        ya = merge_heads(na_latent(qa, ka, va, ka_c, va_c, bias, row_idx))

        qb = apply_rope(rms_norm(heads(gq, GQA_Q_HEADS), q_gain[l]), cos, sin)
        kb = apply_rope(rms_norm(heads(gk, GQA_KV_HEADS), k_gain[l]), cos, sin)
        vb = heads(gv, GQA_KV_HEADS)
        qb_c = rms_norm(heads(gqc, GQA_Q_HEADS), q_gain[l])
        kb_c = rms_norm(heads(gkc, GQA_KV_HEADS), k_gain[l])
        vb_c = heads(gvc, GQA_KV_HEADS)
        k_all = jnp.concatenate([kb, kb_c], axis=2)
        v_all = jnp.concatenate([vb, vb_c], axis=2)
        ob = gqa_latent(qb.reshape(b, GQA_KV_HEADS, GQA_REP, s, HEAD_DIM), k_all, v_all)
        yb = merge_heads(ob.reshape(b, GQA_Q_HEADS, s, HEAD_DIM))

        x = x + g1 * branch_merge(ya, yb, ga, gb, w_pa[l], w_pb[l], w_o[l])

        if not last:
            ya_c = merge_heads(softmax_attn(qa_c[:, :, None], ka_c, va_c)[:, :, 0])
            ob_c = softmax_attn(qb_c.reshape(b, GQA_KV_HEADS, GQA_REP, n_ctx, HEAD_DIM), kb_c, vb_c)
            yb_c = merge_heads(ob_c.reshape(b, GQA_Q_HEADS, n_ctx, HEAD_DIM))
            ctx = ctx + cg1 * branch_merge(ya_c, yb_c, gac, gbc, w_pa[l], w_pb[l], w_o[l])

        h2 = modulate(rms_norm(x, norm2[l]), sh2, sc2)
        x = x + g2 * swiglu(h2, w_ffn_in[l], w_ffn_out[l])
        if not last:
            hc2 = modulate(rms_norm(ctx, norm2[l]), csh2, csc2)
            ctx = ctx + cg2 * swiglu(hc2, w_ffn_in[l], w_ffn_out[l])

    return rms_norm(x, final_norm)
```

```python
import functools
import math

import jax
import jax.numpy as jnp
import numpy as np
from jax import lax
from jax.experimental import pallas as pl
from jax.experimental.pallas import tpu as pltpu

D_MODEL = 1024
DEPTH = 4
GRID_W = 64
CTX_LEN = 256
HEAD_DIM = 64
NA_HEADS = 8
NA_WIN_H = 8
NA_WIN_W = 16
GQA_Q_HEADS = 8
GQA_KV_HEADS = 2
GQA_REP = GQA_Q_HEADS // GQA_KV_HEADS
NA_WIDTH = NA_HEADS * HEAD_DIM
GQA_Q_WIDTH = GQA_Q_HEADS * HEAD_DIM
GQA_KV_WIDTH = GQA_KV_HEADS * HEAD_DIM
D_FF = -(-(8 * D_MODEL) // (3 * 256)) * 256
ROPE_THETA = 10000.0
EPS = 1e-6
SCALE = HEAD_DIM ** -0.5
LOG2E = math.log2(math.e)
QK_SCALE = SCALE * LOG2E

TOK_TILE = 256
NA_ROWS_PER_TILE = TOK_TILE // GRID_W
NA_KEYS_WIN = NA_WIN_H * GRID_W
N_DR = 2 * NA_WIN_H - 1
N_DC = 2 * NA_WIN_W - 1
FF_CHUNK = 256
VMEM_LIMIT_BYTES = 56 * 1024 * 1024

BF16 = jnp.bfloat16
F32 = jnp.float32


def _cparams(n_grid_dims):
    return pltpu.CompilerParams(
        dimension_semantics=("arbitrary",) * n_grid_dims,
        vmem_limit_bytes=VMEM_LIMIT_BYTES)


def _nt_dot(a, b):
    return lax.dot_general(a, b, (((1,), (1,)), ((), ())), preferred_element_type=F32)


def _dot(a, b):
    return jnp.dot(a, b, preferred_element_type=F32)


def _rms_mod(x, g, shift, scale):
    ms = jnp.mean(x * x, axis=-1, keepdims=True)
    y = x * lax.rsqrt(ms + EPS) * g
    return y * (1.0 + scale) + shift


def _mod_kernel(cc_ref, w_ref, b_ref, o_ref):
    cc = cc_ref[...]
    a = (cc * jax.nn.sigmoid(cc)).astype(BF16)
    o_ref[...] = _dot(a, w_ref[...].astype(BF16)) + b_ref[...]


def _mod_call(cc, w_mod, b_mod):
    n_rows = cc.shape[0]
    return pl.pallas_call(
        _mod_kernel,
        grid=(DEPTH, 6),
        in_specs=[
            pl.BlockSpec((n_rows, D_MODEL), lambda l, j: (0, 0)),
            pl.BlockSpec((None, D_MODEL, D_MODEL), lambda l, j: (l, 0, j)),
            pl.BlockSpec((None, 1, D_MODEL), lambda l, j: (l, 0, j)),
        ],
        out_specs=pl.BlockSpec((None, n_rows, D_MODEL), lambda l, j: (l, 0, j)),
        out_shape=jax.ShapeDtypeStruct((DEPTH, n_rows, 6 * D_MODEL), F32),
        compiler_params=_cparams(2),
        name="adaln_mod",
    )(cc, w_mod, b_mod.reshape(DEPTH, 1, 6 * D_MODEL))


def _bias_kernel(rpb_ref, o_ref):
    lh = pl.program_id(0)
    base = lh * (N_DR * N_DC)
    wq = lax.broadcasted_iota(jnp.int32, (GRID_W, 2 * GRID_W), 0)
    lane = lax.broadcasted_iota(jnp.int32, (GRID_W, 2 * GRID_W), 1)
    upper = lane >= GRID_W
    wk = jnp.bitwise_and(lane, GRID_W - 1)
    dci = jnp.clip(wk - wq, -(NA_WIN_W - 1), NA_WIN_W - 1) + (NA_WIN_W - 1)
    c_start = jnp.clip(wq - NA_WIN_W // 2, 0, GRID_W - NA_WIN_W)
    in_win = jnp.logical_and(wk >= c_start, wk < c_start + NA_WIN_W)

    def body(d, carry):
        acc = jnp.zeros((GRID_W, 2 * GRID_W), F32)
        for dc in range(N_DC):
            v_lo = rpb_ref[base + d * N_DC + dc]
            v_hi = rpb_ref[base + (d + 1) * N_DC + dc]
            acc = jnp.where(dci == dc, jnp.where(upper, v_hi, v_lo), acc)
        o_ref[d] = jnp.where(in_win, acc * LOG2E, -jnp.inf)
        return carry

    lax.fori_loop(0, N_DR - 1, body, 0)


def _bias_call(na_rpb):
    rpb_flat = na_rpb.astype(F32).reshape(-1)
    return pl.pallas_call(
        _bias_kernel,
        grid=(DEPTH * NA_HEADS,),
        in_specs=[pl.BlockSpec(memory_space=pltpu.SMEM)],
        out_specs=pl.BlockSpec((None, N_DR - 1, GRID_W, 2 * GRID_W), lambda i: (i, 0, 0, 0)),
        out_shape=jax.ShapeDtypeStruct((DEPTH * NA_HEADS, N_DR - 1, GRID_W, 2 * GRID_W), F32),
        compiler_params=_cparams(1),
        name="na_bias_slabs",
    )(rpb_flat)


def _inproj_kernel(tok_ref, mod_ref, n1_ref, wna_ref, wgt_ref, qg_ref, kg_ref, cos_ref, sin_ref,
                   na_ref, gq_ref, gk_ref, gv_ref):
    x = tok_ref[...]
    h = _rms_mod(x, n1_ref[...], mod_ref[0:1, :], mod_ref[1:2, :]).astype(BF16)

    na = _dot(h, wna_ref[...])
    na_ref[:, 0:NA_WIDTH] = (na[:, 0:NA_WIDTH] * QK_SCALE).astype(BF16)
    na_ref[:, NA_WIDTH:3 * NA_WIDTH] = na[:, NA_WIDTH:3 * NA_WIDTH].astype(BF16)

    gt = _nt_dot(wgt_ref[...], h)
    cos = cos_ref[...]
    sin = sin_ref[...]
    half = HEAD_DIM // 2

    def norm_rope(xh, gain):
        ms = jnp.sum(xh * xh, axis=0, keepdims=True) * (1.0 / HEAD_DIM)
        y = xh * lax.rsqrt(ms + EPS) * gain
        y1 = y[0:half]
        y2 = y[half:HEAD_DIM]
        return jnp.concatenate([y1 * cos - y2 * sin, y1 * sin + y2 * cos], axis=0)

    qg = qg_ref[...]
    for hd in range(GQA_Q_HEADS):
        q = norm_rope(gt[hd * HEAD_DIM:(hd + 1) * HEAD_DIM], qg)
        gq_ref[hd] = (q * QK_SCALE).astype(BF16)
    kg = kg_ref[...]
    for kv in range(GQA_KV_HEADS):
        r0 = GQA_Q_WIDTH + kv * HEAD_DIM
        k = norm_rope(gt[r0:r0 + HEAD_DIM], kg)
        gk_ref[kv] = k.T.astype(BF16)
        r1 = GQA_Q_WIDTH + GQA_KV_WIDTH + kv * HEAD_DIM
        gv_ref[kv] = gt[r1:r1 + HEAD_DIM].astype(BF16)


def _inproj_call(tok, mod, n1, wna, wgt, qg, kg, cos_t, sin_t, n_lat_tiles):
    b, n_tok, _ = tok.shape
    nt = n_tok // TOK_TILE
    return pl.pallas_call(
        _inproj_kernel,
        grid=(b, nt),
        in_specs=[
            pl.BlockSpec((None, TOK_TILE, D_MODEL), lambda bi, i: (bi, i, 0)),
            pl.BlockSpec((None, None, 6, D_MODEL), lambda bi, i: (bi, i // n_lat_tiles, 0, 0)),
            pl.BlockSpec((1, D_MODEL), lambda bi, i: (0, 0)),
            pl.BlockSpec((D_MODEL, 3 * NA_WIDTH), lambda bi, i: (0, 0)),
            pl.BlockSpec((GQA_Q_WIDTH + 2 * GQA_KV_WIDTH, D_MODEL), lambda bi, i: (0, 0)),
            pl.BlockSpec((HEAD_DIM, 1), lambda bi, i: (0, 0)),
            pl.BlockSpec((HEAD_DIM, 1), lambda bi, i: (0, 0)),
            pl.BlockSpec((HEAD_DIM // 2, TOK_TILE), lambda bi, i: (0, i)),
            pl.BlockSpec((HEAD_DIM // 2, TOK_TILE), lambda bi, i: (0, i)),
        ],
        out_specs=[
            pl.BlockSpec((None, TOK_TILE, 3 * NA_WIDTH), lambda bi, i: (bi, i, 0)),
            pl.BlockSpec((None, GQA_Q_HEADS, HEAD_DIM, TOK_TILE), lambda bi, i: (bi, 0, 0, i)),
            pl.BlockSpec((None, GQA_KV_HEADS, TOK_TILE, HEAD_DIM), lambda bi, i: (bi, 0, i, 0)),
            pl.BlockSpec((None, GQA_KV_HEADS, None, HEAD_DIM, TOK_TILE),
                         lambda bi, i: (bi, 0, i, 0, 0)),
        ],
        out_shape=[
            jax.ShapeDtypeStruct((b, n_tok, 3 * NA_WIDTH), BF16),
            jax.ShapeDtypeStruct((b, GQA_Q_HEADS, HEAD_DIM, n_tok), BF16),
            jax.ShapeDtypeStruct((b, GQA_KV_HEADS, n_tok, HEAD_DIM), BF16),
            jax.ShapeDtypeStruct((b, GQA_KV_HEADS, nt, HEAD_DIM, TOK_TILE), BF16),
        ],
        compiler_params=_cparams(2),
        name="in_proj",
    )(tok, mod, n1, wna, wgt, qg, kg, cos_t, sin_t)


def _na_kernel(q_ref, k_ref, v_ref, w_ref, o_ref, *, n_lat_tiles, n_rows):
    i = pl.program_id(2)
    n_lat = n_lat_tiles * TOK_TILE
    lane = lax.broadcasted_iota(jnp.int32, (1, 2 * HEAD_DIM), 1)
    lo_half = lane < HEAD_DIM
    k_ctx = k_ref[n_lat:n_lat + CTX_LEN, :]
    v_ctx = v_ref[n_lat:n_lat + CTX_LEN, :]

    def head_q(q, hh):
        keep = lo_half if hh == 0 else jnp.logical_not(lo_half)
        return jnp.where(keep, q, jnp.zeros_like(q))

    @pl.when(i < n_lat_tiles)
    def _latent():
        for rr in range(NA_ROWS_PER_TILE):
            r = i * NA_ROWS_PER_TILE + rr
            r_start = jnp.clip(r - NA_WIN_H // 2, 0, n_rows - NA_WIN_H)
            d0 = r_start - r + (NA_WIN_H - 1)
            k0 = pl.multiple_of(r_start * GRID_W, GRID_W)
            k_win = k_ref[pl.ds(k0, NA_KEYS_WIN), :]
            v_win = v_ref[pl.ds(k0, NA_KEYS_WIN), :]
            q = q_ref[rr * GRID_W:(rr + 1) * GRID_W, :]
            outs = []
            for hh in range(2):
                qm = head_q(q, hh)
                bias = jnp.concatenate(
                    [w_ref[hh, d0 + 2 * jj] for jj in range(NA_WIN_H // 2)], axis=1)
                s_win = _nt_dot(qm, k_win) + bias
                s_ctx = _nt_dot(qm, k_ctx)
                m = jnp.maximum(jnp.max(s_win, axis=-1, keepdims=True),
                                jnp.max(s_ctx, axis=-1, keepdims=True))
                p_win = jnp.exp2(s_win - m)
                p_ctx = jnp.exp2(s_ctx - m)
                denom = (jnp.sum(p_win, axis=-1, keepdims=True)
                         + jnp.sum(p_ctx, axis=-1, keepdims=True))
                pv = _dot(p_win.astype(BF16), v_win) + _dot(p_ctx.astype(BF16), v_ctx)
                outs.append(pv / denom)
            o_ref[rr * GRID_W:(rr + 1) * GRID_W, :] = jnp.where(
                lo_half, outs[0], outs[1]).astype(BF16)

    @pl.when(i >= n_lat_tiles)
    def _context():
        q = q_ref[...]
        outs = []
        for hh in range(2):
            s = _nt_dot(head_q(q, hh), k_ctx)
            m = jnp.max(s, axis=-1, keepdims=True)
            p = jnp.exp2(s - m)
            denom = jnp.sum(p, axis=-1, keepdims=True)
            outs.append(_dot(p.astype(BF16), v_ctx) / denom)
        o_ref[...] = jnp.where(lo_half, outs[0], outs[1]).astype(BF16)


def _na_call(na_qkv, bias_slabs, n_q_tiles, n_lat_tiles):
    b, n_tok, _ = na_qkv.shape
    n_pairs = NA_HEADS // 2
    kern = functools.partial(_na_kernel, n_lat_tiles=n_lat_tiles,
                             n_rows=n_lat_tiles * NA_ROWS_PER_TILE)
    return pl.pallas_call(
        kern,
        grid=(b, n_pairs, n_q_tiles),
        in_specs=[
            pl.BlockSpec((None, TOK_TILE, 2 * HEAD_DIM), lambda bi, hp, i: (bi, i, hp)),
            pl.BlockSpec((None, n_tok, 2 * HEAD_DIM), lambda bi, hp, i: (bi, 0, n_pairs + hp)),
            pl.BlockSpec((None, n_tok, 2 * HEAD_DIM), lambda bi, hp, i: (bi, 0, 2 * n_pairs + hp)),
            pl.BlockSpec((2, N_DR - 1, GRID_W, 2 * GRID_W), lambda bi, hp, i: (hp, 0, 0, 0)),
        ],
        out_specs=pl.BlockSpec((None, TOK_TILE, 2 * HEAD_DIM), lambda bi, hp, i: (bi, i, hp)),
        out_shape=jax.ShapeDtypeStruct((b, n_q_tiles * TOK_TILE, NA_WIDTH), BF16),
        compiler_params=_cparams(3),
        name="na_attn",
    )(na_qkv, na_qkv, na_qkv, bias_slabs)


def _gqa_kernel(q_ref, k_ref, v_ref, o_ref, m_sc, l_sc, acc_sc, *, n_lat_tiles):
    i = pl.program_id(2)
    n_chunks = n_lat_tiles + 1
    m_sc[...] = jnp.full(m_sc.shape, -jnp.inf, F32)
    l_sc[...] = jnp.zeros(l_sc.shape, F32)
    acc_sc[...] = jnp.zeros(acc_sc.shape, F32)
    c_lo = jnp.where(i >= n_lat_tiles, n_lat_tiles, 0)

    def chunk(c, carry):
        k0 = pl.multiple_of(c * TOK_TILE, TOK_TILE)
        k_c = k_ref[pl.ds(k0, TOK_TILE), :]
        v_c = v_ref[c]
        for hd in range(GQA_REP):
            s = _dot(k_c, q_ref[hd])
            m_old = m_sc[hd]
            m_new = jnp.maximum(m_old, jnp.max(s, axis=0, keepdims=True))
            alpha = jnp.exp2(m_old - m_new)
            p = jnp.exp2(s - m_new)
            l_sc[hd] = l_sc[hd] * alpha + jnp.sum(p, axis=0, keepdims=True)
            acc_sc[hd] = acc_sc[hd] * alpha + _dot(v_c, p.astype(BF16))
            m_sc[hd] = m_new
        return carry

    lax.fori_loop(c_lo, n_chunks, chunk, 0)
    outs = [acc_sc[hd] / l_sc[hd] for hd in range(GQA_REP)]
    o_ref[...] = jnp.concatenate(outs, axis=0).T.astype(BF16)


def _gqa_call(gq, gk, gv, n_q_tiles, n_lat_tiles):
    b = gq.shape[0]
    n_tok = gq.shape[-1]
    n_key_tiles = gv.shape[2]
    kern = functools.partial(_gqa_kernel, n_lat_tiles=n_lat_tiles)
    return pl.pallas_call(
        kern,
        grid=(b, GQA_KV_HEADS, n_q_tiles),
        in_specs=[
            pl.BlockSpec((None, GQA_REP, HEAD_DIM, TOK_TILE), lambda bi, g, i: (bi, g, 0, i)),
            pl.BlockSpec((None, None, n_tok, HEAD_DIM), lambda bi, g, i: (bi, g, 0, 0)),
            pl.BlockSpec((None, None, n_key_tiles, HEAD_DIM, TOK_TILE),
                         lambda bi, g, i: (bi, g, 0, 0, 0)),
        ],
        out_specs=pl.BlockSpec((None, TOK_TILE, GQA_REP * HEAD_DIM), lambda bi, g, i: (bi, i, g)),
        out_shape=jax.ShapeDtypeStruct((b, n_q_tiles * TOK_TILE, GQA_Q_WIDTH), BF16),
        scratch_shapes=[
            pltpu.VMEM((GQA_REP, 1, TOK_TILE), F32),
            pltpu.VMEM((GQA_REP, 1, TOK_TILE), F32),
            pltpu.VMEM((GQA_REP, HEAD_DIM, TOK_TILE), F32),
        ],
        compiler_params=_cparams(3),
        name="gqa_attn",
    )(gq, gk, gv)


def _merge_kernel(tok_ref, mod_ref, n1_ref, ya_ref, yb_ref, wg_ref, wpa_ref, wpb_ref, wo_ref, o_ref):
    x = tok_ref[...]
    h = _rms_mod(x, n1_ref[...], mod_ref[0:1, :], mod_ref[1:2, :]).astype(BF16)
    gates = _dot(h, wg_ref[...])
    pa = _dot(ya_ref[...], wpa_ref[...])
    pb = _dot(yb_ref[...], wpb_ref[...])
    merged = (jax.nn.sigmoid(gates[:, 0:D_MODEL]) * pa
              + jax.nn.sigmoid(gates[:, D_MODEL:2 * D_MODEL]) * pb)
    y = _dot(merged.astype(BF16), wo_ref[...])
    o_ref[...] = x + mod_ref[2:3, :] * y


def _merge_call(tok, mod, n1, ya, yb, wg, wpa, wpb, wo, n_tiles, n_lat_tiles):
    b = tok.shape[0]
    full = lambda shp: pl.BlockSpec(shp, lambda bi, i: (0,) * len(shp))
    return pl.pallas_call(
        _merge_kernel,
        grid=(b, n_tiles),
        in_specs=[
            pl.BlockSpec((None, TOK_TILE, D_MODEL), lambda bi, i: (bi, i, 0)),
            pl.BlockSpec((None, None, 6, D_MODEL), lambda bi, i: (bi, i // n_lat_tiles, 0, 0)),
            full((1, D_MODEL)),
            pl.BlockSpec((None, TOK_TILE, NA_WIDTH), lambda bi, i: (bi, i, 0)),
            pl.BlockSpec((None, TOK_TILE, GQA_Q_WIDTH), lambda bi, i: (bi, i, 0)),
            full((D_MODEL, 2 * D_MODEL)),
            full((NA_WIDTH, D_MODEL)),
            full((GQA_Q_WIDTH, D_MODEL)),
            full((D_MODEL, D_MODEL)),
        ],
        out_specs=pl.BlockSpec((None, TOK_TILE, D_MODEL), lambda bi, i: (bi, i, 0)),
        out_shape=jax.ShapeDtypeStruct((b, n_tiles * TOK_TILE, D_MODEL), F32),
        compiler_params=_cparams(2),
        name="branch_merge",
    )(tok, mod, n1, ya, yb, wg, wpa, wpb, wo)


def _ffn_kernel(x_ref, mod_ref, n2_ref, win_ref, wout_ref, fn_ref, o_ref, g_sc, *, final_norm):
    x = x_ref[...]
    h = _rms_mod(x, n2_ref[...], mod_ref[3:4, :], mod_ref[4:5, :]).astype(BF16)
    for j in range(D_FF // FF_CHUNK):
        c0 = j * FF_CHUNK
        a = _dot(h, win_ref[:, c0:c0 + FF_CHUNK])
        u = _dot(h, win_ref[:, D_FF + c0:D_FF + c0 + FF_CHUNK])
        g_sc[:, c0:c0 + FF_CHUNK] = (a * jax.nn.sigmoid(a) * u).astype(BF16)
    y = x + mod_ref[5:6, :] * _dot(g_sc[...], wout_ref[...])
    if final_norm:
        ms = jnp.mean(y * y, axis=-1, keepdims=True)
        y = y * lax.rsqrt(ms + EPS) * fn_ref[...]
    o_ref[...] = y


def _ffn_call(x1, mod, n2, win, wout, fn, final_norm, n_lat_tiles):
    b, n_tok, _ = x1.shape
    n_tiles = n_tok // TOK_TILE
    full = lambda shp: pl.BlockSpec(shp, lambda bi, i: (0,) * len(shp))
    kern = functools.partial(_ffn_kernel, final_norm=final_norm)
    return pl.pallas_call(
        kern,
        grid=(b, n_tiles),
        in_specs=[
            pl.BlockSpec((None, TOK_TILE, D_MODEL), lambda bi, i: (bi, i, 0)),
            pl.BlockSpec((None, None, 6, D_MODEL), lambda bi, i: (bi, i // n_lat_tiles, 0, 0)),
            full((1, D_MODEL)),
            full((D_MODEL, 2 * D_FF)),
            full((D_FF, D_MODEL)),
            full((1, D_MODEL)),
        ],
        out_specs=pl.BlockSpec((None, TOK_TILE, D_MODEL), lambda bi, i: (bi, i, 0)),
        out_shape=jax.ShapeDtypeStruct((b, n_tok, D_MODEL), F32),
        scratch_shapes=[pltpu.VMEM((TOK_TILE, D_FF), BF16)],
        compiler_params=_cparams(2),
        name="swiglu_ffn",
    )(x1, mod, n2, win, wout, fn)


def _rope_tables(n_lat, n_ctx):
    t = jnp.arange(n_lat)
    row = (t // GRID_W).astype(F32)
    col = (t % GRID_W).astype(F32)
    half = HEAD_DIM // 2
    inv = ROPE_THETA ** (-jnp.arange(0, half, 2, dtype=F32) / half)
    ang = jnp.concatenate([row[:, None] * inv, col[:, None] * inv], axis=-1)
    cos_t = jnp.concatenate([jnp.cos(ang).T, jnp.ones((half, n_ctx), F32)], axis=1)
    sin_t = jnp.concatenate([jnp.sin(ang).T, jnp.zeros((half, n_ctx), F32)], axis=1)
    return cos_t, sin_t


def kernel(x, c, ctx, c_ctx, w_mod, b_mod, norm1, w_in, na_rpb, q_gain, k_gain, w_pa, w_pb, w_o,
           norm2, w_ffn_in, w_ffn_out, final_norm):
    b, s, d = x.shape
    n_ctx = ctx.shape[1]
    assert d == D_MODEL and n_ctx == CTX_LEN == TOK_TILE and s % TOK_TILE == 0
    assert s // GRID_W >= NA_WIN_H
    n_lat_tiles = s // TOK_TILE
    n_tiles = n_lat_tiles + 1

    perm = np.concatenate([np.arange(0, HEAD_DIM, 2), np.arange(1, HEAD_DIM, 2)])
    q_cols = (np.arange(GQA_Q_HEADS)[:, None] * HEAD_DIM + perm[None, :]).reshape(-1)
    k_cols = (np.arange(GQA_KV_HEADS)[:, None] * HEAD_DIM + perm[None, :]).reshape(-1)
    o_gq = 3 * NA_WIDTH
    o_gk = o_gq + GQA_Q_WIDTH
    o_gv = o_gk + GQA_KV_WIDTH
    o_ga = o_gv + GQA_KV_WIDTH

    cos_t, sin_t = _rope_tables(s, n_ctx)
    tok = jnp.concatenate([x, ctx], axis=1)

    n_rows_cc = 8
    cc = jnp.zeros((n_rows_cc, D_MODEL), F32).at[0:b].set(c).at[b].set(c_ctx)
    mod_all = _mod_call(cc, w_mod, b_mod)
    mod_lat = mod_all[:, 0:b].reshape(DEPTH, b, 1, 6, D_MODEL)
    mod_ctx = jnp.broadcast_to(mod_all[:, b].reshape(DEPTH, 1, 1, 6, D_MODEL),
                               (DEPTH, b, 1, 6, D_MODEL))
    mod_sel = jnp.concatenate([mod_lat, mod_ctx], axis=2)

    bias_slabs = _bias_call(na_rpb).reshape(DEPTH, NA_HEADS, N_DR - 1, GRID_W, 2 * GRID_W)

    out = None
    for l in range(DEPTH):
        last = l == DEPTH - 1
        w_l = w_in[l]
        wna = w_l[:, 0:o_gq].astype(BF16)
        wgt = jnp.concatenate(
            [w_l[:, o_gq:o_gk][:, q_cols], w_l[:, o_gk:o_gv][:, k_cols], w_l[:, o_gv:o_ga]],
            axis=1).T.astype(BF16)
        wg = w_l[:, o_ga:].astype(BF16)
        n1 = norm1[l].reshape(1, D_MODEL)
        n2 = norm2[l].reshape(1, D_MODEL)
        qg = q_gain[l][perm].reshape(HEAD_DIM, 1)
        kg = k_gain[l][perm].reshape(HEAD_DIM, 1)
        mod = mod_sel[l]

        na_qkv, gq, gk, gv = _inproj_call(tok, mod, n1, wna, wgt, qg, kg, cos_t, sin_t, n_lat_tiles)
        n_q_tiles = n_lat_tiles if last else n_tiles
        ya = _na_call(na_qkv, bias_slabs[l], n_q_tiles, n_lat_tiles)
        yb = _gqa_call(gq, gk, gv, n_q_tiles, n_lat_tiles)
        x1 = _merge_call(tok, mod, n1, ya, yb, wg, w_pa[l].astype(BF16), w_pb[l].astype(BF16),
                         w_o[l].astype(BF16), n_q_tiles, n_lat_tiles)
        tok_next = _ffn_call(x1, mod, n2, w_ffn_in[l].astype(BF16), w_ffn_out[l].astype(BF16),
                             final_norm.reshape(1, D_MODEL), last, n_lat_tiles)
        if last:
            out = tok_next
        else:
            tok = tok_next
    return out
```

```python
import functools
import math

import jax
import jax.numpy as jnp
import numpy as np
from jax import lax
from jax.experimental import pallas as pl
from jax.experimental.pallas import tpu as pltpu

D_MODEL = 1024
DEPTH = 4
GRID_W = 64
CTX_LEN = 256
HEAD_DIM = 64
NA_HEADS = 8
NA_WIN_H = 8
NA_WIN_W = 16
GQA_Q_HEADS = 8
GQA_KV_HEADS = 2
GQA_REP = GQA_Q_HEADS // GQA_KV_HEADS
NA_WIDTH = NA_HEADS * HEAD_DIM
GQA_Q_WIDTH = GQA_Q_HEADS * HEAD_DIM
GQA_KV_WIDTH = GQA_KV_HEADS * HEAD_DIM
D_FF = -(-(8 * D_MODEL) // (3 * 256)) * 256
ROPE_THETA = 10000.0
EPS = 1e-6
SCALE = HEAD_DIM ** -0.5
LOG2E = math.log2(math.e)
QK_SCALE = SCALE * LOG2E

TOK_TILE = 256
NA_ROWS_PER_TILE = TOK_TILE // GRID_W
NA_KEYS_WIN = NA_WIN_H * GRID_W
N_DR = 2 * NA_WIN_H - 1
N_DC = 2 * NA_WIN_W - 1
FF_CHUNK = 256
BF16_SUBLANES = 16
V_ROWS = HEAD_DIM + BF16_SUBLANES
VMEM_LIMIT_BYTES = 56 * 1024 * 1024

BF16 = jnp.bfloat16
F32 = jnp.float32


def _cparams(n_grid_dims):
    return pltpu.CompilerParams(
        dimension_semantics=("arbitrary",) * n_grid_dims,
        vmem_limit_bytes=VMEM_LIMIT_BYTES)


def _nt_dot(a, b):
    return lax.dot_general(a, b, (((1,), (1,)), ((), ())), preferred_element_type=F32)


def _dot(a, b):
    return jnp.dot(a, b, preferred_element_type=F32)


def _rms_mod(x, g, shift, scale):
    ms = jnp.mean(x * x, axis=-1, keepdims=True)
    y = x * lax.rsqrt(ms + EPS) * g
    return y * (1.0 + scale) + shift


def _mod_kernel(cc_ref, w_ref, b_ref, o_ref):
    cc = cc_ref[...]
    a = (cc * jax.nn.sigmoid(cc)).astype(BF16)
    o_ref[...] = _dot(a, w_ref[...].astype(BF16)) + b_ref[...]


def _mod_call(cc, w_mod, b_mod):
    n_rows = cc.shape[0]
    return pl.pallas_call(
        _mod_kernel,
        grid=(DEPTH, 6),
        in_specs=[
            pl.BlockSpec((n_rows, D_MODEL), lambda l, j: (0, 0)),
            pl.BlockSpec((None, D_MODEL, D_MODEL), lambda l, j: (l, 0, j)),
            pl.BlockSpec((None, 1, D_MODEL), lambda l, j: (l, 0, j)),
        ],
        out_specs=pl.BlockSpec((None, n_rows, D_MODEL), lambda l, j: (l, 0, j)),
        out_shape=jax.ShapeDtypeStruct((DEPTH, n_rows, 6 * D_MODEL), F32),
        compiler_params=_cparams(2),
        name="adaln_mod",
    )(cc, w_mod, b_mod.reshape(DEPTH, 1, 6 * D_MODEL))


def _bias_kernel(rpb_ref, o_ref):
    lh = pl.program_id(0)
    base = lh * (N_DR * N_DC)
    wq = lax.broadcasted_iota(jnp.int32, (GRID_W, 2 * GRID_W), 0)
    lane = lax.broadcasted_iota(jnp.int32, (GRID_W, 2 * GRID_W), 1)
    upper = lane >= GRID_W
    wk = jnp.bitwise_and(lane, GRID_W - 1)
    dci = jnp.clip(wk - wq, -(NA_WIN_W - 1), NA_WIN_W - 1) + (NA_WIN_W - 1)
    c_start = jnp.clip(wq - NA_WIN_W // 2, 0, GRID_W - NA_WIN_W)
    in_win = jnp.logical_and(wk >= c_start, wk < c_start + NA_WIN_W)

    def body(d, carry):
        acc = jnp.zeros((GRID_W, 2 * GRID_W), F32)
        for dc in range(N_DC):
            v_lo = rpb_ref[base + d * N_DC + dc]
            v_hi = rpb_ref[base + (d + 1) * N_DC + dc]
            acc = jnp.where(dci == dc, jnp.where(upper, v_hi, v_lo), acc)
        o_ref[d] = jnp.where(in_win, acc * LOG2E, -jnp.inf)
        return carry

    lax.fori_loop(0, N_DR - 1, body, 0)


def _bias_call(na_rpb):
    rpb_flat = na_rpb.astype(F32).reshape(-1)
    return pl.pallas_call(
        _bias_kernel,
        grid=(DEPTH * NA_HEADS,),
        in_specs=[pl.BlockSpec(memory_space=pltpu.SMEM)],
        out_specs=pl.BlockSpec((None, N_DR - 1, GRID_W, 2 * GRID_W), lambda i: (i, 0, 0, 0)),
        out_shape=jax.ShapeDtypeStruct((DEPTH * NA_HEADS, N_DR - 1, GRID_W, 2 * GRID_W), F32),
        compiler_params=_cparams(1),
        name="na_bias_slabs",
    )(rpb_flat)


def _inproj_kernel(tok_ref, mod_ref, n1_ref, wna_ref, wgt_ref, qg_ref, kg_ref, cos_ref, sin_ref,
                   na_ref, gq_ref, gk_ref, gv_ref):
    x = tok_ref[...]
    h = _rms_mod(x, n1_ref[...], mod_ref[0:1, :], mod_ref[1:2, :]).astype(BF16)

    na = _dot(h, wna_ref[...])
    na_ref[:, 0:NA_WIDTH] = (na[:, 0:NA_WIDTH] * QK_SCALE).astype(BF16)
    na_ref[:, NA_WIDTH:3 * NA_WIDTH] = na[:, NA_WIDTH:3 * NA_WIDTH].astype(BF16)

    gt = _nt_dot(wgt_ref[...], h)
    cos = cos_ref[...]
    sin = sin_ref[...]
    half = HEAD_DIM // 2

    def norm_rope(xh, gain):
        ms = jnp.sum(xh * xh, axis=0, keepdims=True) * (1.0 / HEAD_DIM)
        y = xh * lax.rsqrt(ms + EPS) * gain
        y1 = y[0:half]
        y2 = y[half:HEAD_DIM]
        return jnp.concatenate([y1 * cos - y2 * sin, y1 * sin + y2 * cos], axis=0)

    qg = qg_ref[...]
    for hd in range(GQA_Q_HEADS):
        q = norm_rope(gt[hd * HEAD_DIM:(hd + 1) * HEAD_DIM], qg)
        gq_ref[hd] = (q * QK_SCALE).astype(BF16)
    kg = kg_ref[...]
    for kv in range(GQA_KV_HEADS):
        r0 = GQA_Q_WIDTH + kv * HEAD_DIM
        k = norm_rope(gt[r0:r0 + HEAD_DIM], kg)
        gk_ref[kv] = k.T.astype(BF16)
        r1 = GQA_Q_WIDTH + GQA_KV_WIDTH + kv * HEAD_DIM
        gv_ref[kv, 0:HEAD_DIM] = gt[r1:r1 + HEAD_DIM].astype(BF16)
        gv_ref[kv, HEAD_DIM:V_ROWS] = jnp.ones((V_ROWS - HEAD_DIM, TOK_TILE), BF16)


def _inproj_call(tok, mod, n1, wna, wgt, qg, kg, cos_t, sin_t, n_lat_tiles):
    b, n_tok, _ = tok.shape
    nt = n_tok // TOK_TILE
    return pl.pallas_call(
        _inproj_kernel,
        grid=(b, nt),
        in_specs=[
            pl.BlockSpec((None, TOK_TILE, D_MODEL), lambda bi, i: (bi, i, 0)),
            pl.BlockSpec((None, None, 6, D_MODEL), lambda bi, i: (bi, i // n_lat_tiles, 0, 0)),
            pl.BlockSpec((1, D_MODEL), lambda bi, i: (0, 0)),
            pl.BlockSpec((D_MODEL, 3 * NA_WIDTH), lambda bi, i: (0, 0)),
            pl.BlockSpec((GQA_Q_WIDTH + 2 * GQA_KV_WIDTH, D_MODEL), lambda bi, i: (0, 0)),
            pl.BlockSpec((HEAD_DIM, 1), lambda bi, i: (0, 0)),
            pl.BlockSpec((HEAD_DIM, 1), lambda bi, i: (0, 0)),
            pl.BlockSpec((HEAD_DIM // 2, TOK_TILE), lambda bi, i: (0, i)),
            pl.BlockSpec((HEAD_DIM // 2, TOK_TILE), lambda bi, i: (0, i)),
        ],
        out_specs=[
            pl.BlockSpec((None, TOK_TILE, 3 * NA_WIDTH), lambda bi, i: (bi, i, 0)),
            pl.BlockSpec((None, GQA_Q_HEADS, HEAD_DIM, TOK_TILE), lambda bi, i: (bi, 0, 0, i)),
            pl.BlockSpec((None, GQA_KV_HEADS, TOK_TILE, HEAD_DIM), lambda bi, i: (bi, 0, i, 0)),
            pl.BlockSpec((None, GQA_KV_HEADS, None, V_ROWS, TOK_TILE),
                         lambda bi, i: (bi, 0, i, 0, 0)),
        ],
        out_shape=[
            jax.ShapeDtypeStruct((b, n_tok, 3 * NA_WIDTH), BF16),
            jax.ShapeDtypeStruct((b, GQA_Q_HEADS, HEAD_DIM, n_tok), BF16),
            jax.ShapeDtypeStruct((b, GQA_KV_HEADS, n_tok, HEAD_DIM), BF16),
            jax.ShapeDtypeStruct((b, GQA_KV_HEADS, nt, V_ROWS, TOK_TILE), BF16),
        ],
        compiler_params=_cparams(2),
        name="in_proj",
    )(tok, mod, n1, wna, wgt, qg, kg, cos_t, sin_t)


def _na_kernel(q_ref, k_ref, v_ref, w_ref, o_ref, *, n_lat_tiles, n_rows):
    i = pl.program_id(2)
    n_lat = n_lat_tiles * TOK_TILE
    lane = lax.broadcasted_iota(jnp.int32, (1, 2 * HEAD_DIM), 1)
    lo_half = lane < HEAD_DIM
    k_ctx = k_ref[n_lat:n_lat + CTX_LEN, :]
    v_ctx = v_ref[n_lat:n_lat + CTX_LEN, :]

    q = q_ref[...]
    zero = jnp.zeros_like(q)
    q2 = jnp.concatenate([jnp.where(lo_half, q, zero), jnp.where(lo_half, zero, q)], axis=0)

    def both_heads(x, rr):
        lo = rr * GRID_W
        return jnp.concatenate([x[lo:lo + GRID_W], x[TOK_TILE + lo:TOK_TILE + lo + GRID_W]], axis=0)

    s_ctx = _nt_dot(q2, k_ctx)

    @pl.when(i < n_lat_tiles)
    def _latent():
        v_wins = []
        s_wins = []
        for rr in range(NA_ROWS_PER_TILE):
            r = i * NA_ROWS_PER_TILE + rr
            r_start = jnp.clip(r - NA_WIN_H // 2, 0, n_rows - NA_WIN_H)
            d0 = r_start - r + (NA_WIN_H - 1)
            k0 = pl.multiple_of(r_start * GRID_W, GRID_W)
            k_win = k_ref[pl.ds(k0, NA_KEYS_WIN), :]
            v_wins.append(v_ref[pl.ds(k0, NA_KEYS_WIN), :])
            bias = jnp.concatenate(
                [jnp.concatenate([w_ref[hh, d0 + 2 * jj] for jj in range(NA_WIN_H // 2)], axis=1)
                 for hh in range(2)], axis=0)
            s_wins.append(_nt_dot(both_heads(q2, rr), k_win) + bias)
        p_wins = []
        p_ctxs = []
        denoms = []
        for rr in range(NA_ROWS_PER_TILE):
            s_w = s_wins[rr]
            s_c = both_heads(s_ctx, rr)
            m = jnp.maximum(jnp.max(s_w, axis=-1, keepdims=True),
                            jnp.max(s_c, axis=-1, keepdims=True))
            p_w = jnp.exp2(s_w - m)
            p_c = jnp.exp2(s_c - m)
            denoms.append(jnp.sum(p_w, axis=-1, keepdims=True)
                          + jnp.sum(p_c, axis=-1, keepdims=True))
            p_wins.append(p_w.astype(BF16))
            p_ctxs.append(p_c.astype(BF16))
        p_ctx = jnp.concatenate([p[0:GRID_W] for p in p_ctxs] + [p[GRID_W:] for p in p_ctxs], axis=0)
        o_ctx = _dot(p_ctx, v_ctx)
        for rr in range(NA_ROWS_PER_TILE):
            o = (_dot(p_wins[rr], v_wins[rr]) + both_heads(o_ctx, rr)) / denoms[rr]
            o_ref[rr * GRID_W:(rr + 1) * GRID_W, :] = jnp.where(
                lo_half, o[0:GRID_W], o[GRID_W:]).astype(BF16)

    @pl.when(i >= n_lat_tiles)
    def _context():
        m = jnp.max(s_ctx, axis=-1, keepdims=True)
        p = jnp.exp2(s_ctx - m)
        o = _dot(p.astype(BF16), v_ctx) / jnp.sum(p, axis=-1, keepdims=True)
        o_ref[...] = jnp.where(lo_half, o[0:TOK_TILE], o[TOK_TILE:]).astype(BF16)


def _na_call(na_qkv, bias_slabs, n_q_tiles, n_lat_tiles):
    b, n_tok, _ = na_qkv.shape
    n_pairs = NA_HEADS // 2
    kern = functools.partial(_na_kernel, n_lat_tiles=n_lat_tiles,
                             n_rows=n_lat_tiles * NA_ROWS_PER_TILE)
    return pl.pallas_call(
        kern,
        grid=(b, n_pairs, n_q_tiles),
        in_specs=[
            pl.BlockSpec((None, TOK_TILE, 2 * HEAD_DIM), lambda bi, hp, i: (bi, i, hp)),
            pl.BlockSpec((None, n_tok, 2 * HEAD_DIM), lambda bi, hp, i: (bi, 0, n_pairs + hp)),
            pl.BlockSpec((None, n_tok, 2 * HEAD_DIM), lambda bi, hp, i: (bi, 0, 2 * n_pairs + hp)),
            pl.BlockSpec((2, N_DR - 1, GRID_W, 2 * GRID_W), lambda bi, hp, i: (hp, 0, 0, 0)),
        ],
        out_specs=pl.BlockSpec((None, TOK_TILE, 2 * HEAD_DIM), lambda bi, hp, i: (bi, i, hp)),
        out_shape=jax.ShapeDtypeStruct((b, n_q_tiles * TOK_TILE, NA_WIDTH), BF16),
        compiler_params=_cparams(3),
        name="na_attn",
    )(na_qkv, na_qkv, na_qkv, bias_slabs)


def _gqa_kernel(q_ref, k_ref, v_ref, o_ref, s_a, s_b, cm_a, cm_b, p_a, p_b, m_sc, acc_sc, *,
                n_lat_tiles):
    i = pl.program_id(2)

    def key_tile(c):
        return k_ref[pl.ds(pl.multiple_of(c * TOK_TILE, TOK_TILE), TOK_TILE), :]

    def finish(accs):
        outs = [a[0:HEAD_DIM] / a[HEAD_DIM:HEAD_DIM + 1] for a in accs]
        o_ref[...] = jnp.concatenate(outs, axis=0).T.astype(BF16)

    def step(c_prev, c_next, s_cur, cm_cur, s_nxt, cm_nxt, p_prev, p_cur):
        v_prev = v_ref[c_prev]
        k_next = key_tile(c_next)
        for hd in range(GQA_REP):
            pv = _dot(v_prev, p_prev[hd])
            s = _dot(k_next, q_ref[hd])
            s_nxt[hd] = s
            cm_nxt[hd] = jnp.max(s, axis=0, keepdims=True)
            m_old = m_sc[hd]
            m_new = jnp.maximum(m_old, cm_cur[hd])
            alpha = jnp.exp2(m_old - m_new)
            p_cur[hd] = jnp.exp2(s_cur[hd] - m_new).astype(BF16)
            acc_sc[hd] = (acc_sc[hd] + pv) * alpha
            m_sc[hd] = m_new

    @pl.when(i < n_lat_tiles)
    def _latent():
        m_sc[...] = jnp.full(m_sc.shape, -jnp.inf, F32)
        acc_sc[...] = jnp.zeros(acc_sc.shape, F32)
        p_b[...] = jnp.zeros(p_b.shape, BF16)
        k_0 = key_tile(0)
        for hd in range(GQA_REP):
            s = _dot(k_0, q_ref[hd])
            s_a[hd] = s
            cm_a[hd] = jnp.max(s, axis=0, keepdims=True)

        def pair(j, carry):
            c = 2 * j
            step(jnp.maximum(c - 1, 0), c + 1, s_a, cm_a, s_b, cm_b, p_b, p_a)
            step(c, c + 2, s_b, cm_b, s_a, cm_a, p_a, p_b)
            return carry

        lax.fori_loop(0, n_lat_tiles // 2, pair, 0, unroll=8)
        v_last = v_ref[n_lat_tiles - 1]
        v_ctx = v_ref[n_lat_tiles]
        accs = []
        for hd in range(GQA_REP):
            pv = _dot(v_last, p_b[hd])
            m_old = m_sc[hd]
            m_new = jnp.maximum(m_old, cm_a[hd])
            alpha = jnp.exp2(m_old - m_new)
            p = jnp.exp2(s_a[hd] - m_new).astype(BF16)
            accs.append((acc_sc[hd] + pv) * alpha + _dot(v_ctx, p))
        finish(accs)

    @pl.when(i >= n_lat_tiles)
    def _context():
        k_c = key_tile(n_lat_tiles)
        v_c = v_ref[n_lat_tiles]
        accs = []
        for hd in range(GQA_REP):
            s = _dot(k_c, q_ref[hd])
            p = jnp.exp2(s - jnp.max(s, axis=0, keepdims=True)).astype(BF16)
            accs.append(_dot(v_c, p))
        finish(accs)


def _gqa_call(gq, gk, gv, n_q_tiles, n_lat_tiles):
    b = gq.shape[0]
    n_tok = gq.shape[-1]
    n_key_tiles = gv.shape[2]
    assert n_lat_tiles % 2 == 0 and n_key_tiles == n_lat_tiles + 1
    kern = functools.partial(_gqa_kernel, n_lat_tiles=n_lat_tiles)
    score_buf = pltpu.VMEM((GQA_REP, TOK_TILE, TOK_TILE), F32)
    stat_buf = pltpu.VMEM((GQA_REP, 1, TOK_TILE), F32)
    prob_buf = pltpu.VMEM((GQA_REP, TOK_TILE, TOK_TILE), BF16)
    return pl.pallas_call(
        kern,
        grid=(b, GQA_KV_HEADS, n_q_tiles),
        in_specs=[
            pl.BlockSpec((None, GQA_REP, HEAD_DIM, TOK_TILE), lambda bi, g, i: (bi, g, 0, i)),
            pl.BlockSpec((None, None, n_tok, HEAD_DIM), lambda bi, g, i: (bi, g, 0, 0)),
            pl.BlockSpec((None, None, n_key_tiles, V_ROWS, TOK_TILE),
                         lambda bi, g, i: (bi, g, 0, 0, 0)),
        ],
        out_specs=pl.BlockSpec((None, TOK_TILE, GQA_REP * HEAD_DIM), lambda bi, g, i: (bi, i, g)),
        out_shape=jax.ShapeDtypeStruct((b, n_q_tiles * TOK_TILE, GQA_Q_WIDTH), BF16),
        scratch_shapes=[score_buf, score_buf, stat_buf, stat_buf, prob_buf, prob_buf, stat_buf,
                        pltpu.VMEM((GQA_REP, V_ROWS, TOK_TILE), F32)],
        compiler_params=_cparams(3),
        name="gqa_attn",
    )(gq, gk, gv)


def _merge_kernel(tok_ref, mod_ref, n1_ref, ya_ref, yb_ref, wg_ref, wpa_ref, wpb_ref, wo_ref, o_ref):
    x = tok_ref[...]
    h = _rms_mod(x, n1_ref[...], mod_ref[0:1, :], mod_ref[1:2, :]).astype(BF16)
    gates = _dot(h, wg_ref[...])
    pa = _dot(ya_ref[...], wpa_ref[...])
    pb = _dot(yb_ref[...], wpb_ref[...])
    merged = (jax.nn.sigmoid(gates[:, 0:D_MODEL]) * pa
              + jax.nn.sigmoid(gates[:, D_MODEL:2 * D_MODEL]) * pb)
    y = _dot(merged.astype(BF16), wo_ref[...])
    o_ref[...] = x + mod_ref[2:3, :] * y


def _merge_call(tok, mod, n1, ya, yb, wg, wpa, wpb, wo, n_tiles, n_lat_tiles):
    b = tok.shape[0]
    full = lambda shp: pl.BlockSpec(shp, lambda bi, i: (0,) * len(shp))
    return pl.pallas_call(
        _merge_kernel,
        grid=(b, n_tiles),
        in_specs=[
            pl.BlockSpec((None, TOK_TILE, D_MODEL), lambda bi, i: (bi, i, 0)),
            pl.BlockSpec((None, None, 6, D_MODEL), lambda bi, i: (bi, i // n_lat_tiles, 0, 0)),
            full((1, D_MODEL)),
            pl.BlockSpec((None, TOK_TILE, NA_WIDTH), lambda bi, i: (bi, i, 0)),
            pl.BlockSpec((None, TOK_TILE, GQA_Q_WIDTH), lambda bi, i: (bi, i, 0)),
            full((D_MODEL, 2 * D_MODEL)),
            full((NA_WIDTH, D_MODEL)),
            full((GQA_Q_WIDTH, D_MODEL)),
            full((D_MODEL, D_MODEL)),
        ],
        out_specs=pl.BlockSpec((None, TOK_TILE, D_MODEL), lambda bi, i: (bi, i, 0)),
        out_shape=jax.ShapeDtypeStruct((b, n_tiles * TOK_TILE, D_MODEL), F32),
        compiler_params=_cparams(2),
        name="branch_merge",
    )(tok, mod, n1, ya, yb, wg, wpa, wpb, wo)


def _ffn_kernel(x_ref, mod_ref, n2_ref, win_ref, wout_ref, fn_ref, o_ref, g_sc, *, final_norm):
    x = x_ref[...]
    h = _rms_mod(x, n2_ref[...], mod_ref[3:4, :], mod_ref[4:5, :]).astype(BF16)
    for j in range(D_FF // FF_CHUNK):
        c0 = j * FF_CHUNK
        a = _dot(h, win_ref[:, c0:c0 + FF_CHUNK])
        u = _dot(h, win_ref[:, D_FF + c0:D_FF + c0 + FF_CHUNK])
        g_sc[:, c0:c0 + FF_CHUNK] = (a * jax.nn.sigmoid(a) * u).astype(BF16)
    y = x + mod_ref[5:6, :] * _dot(g_sc[...], wout_ref[...])
    if final_norm:
        ms = jnp.mean(y * y, axis=-1, keepdims=True)
        y = y * lax.rsqrt(ms + EPS) * fn_ref[...]
    o_ref[...] = y


def _ffn_call(x1, mod, n2, win, wout, fn, final_norm, n_lat_tiles):
    b, n_tok, _ = x1.shape
    n_tiles = n_tok // TOK_TILE
    full = lambda shp: pl.BlockSpec(shp, lambda bi, i: (0,) * len(shp))
    kern = functools.partial(_ffn_kernel, final_norm=final_norm)
    return pl.pallas_call(
        kern,
        grid=(b, n_tiles),
        in_specs=[
            pl.BlockSpec((None, TOK_TILE, D_MODEL), lambda bi, i: (bi, i, 0)),
            pl.BlockSpec((None, None, 6, D_MODEL), lambda bi, i: (bi, i // n_lat_tiles, 0, 0)),
            full((1, D_MODEL)),
            full((D_MODEL, 2 * D_FF)),
            full((D_FF, D_MODEL)),
            full((1, D_MODEL)),
        ],
        out_specs=pl.BlockSpec((None, TOK_TILE, D_MODEL), lambda bi, i: (bi, i, 0)),
        out_shape=jax.ShapeDtypeStruct((b, n_tok, D_MODEL), F32),
        scratch_shapes=[pltpu.VMEM((TOK_TILE, D_FF), BF16)],
        compiler_params=_cparams(2),
        name="swiglu_ffn",
    )(x1, mod, n2, win, wout, fn)


def _rope_tables(n_lat, n_ctx):
    t = jnp.arange(n_lat)
    row = (t // GRID_W).astype(F32)
    col = (t % GRID_W).astype(F32)
    half = HEAD_DIM // 2
    inv = ROPE_THETA ** (-jnp.arange(0, half, 2, dtype=F32) / half)
    ang = jnp.concatenate([row[:, None] * inv, col[:, None] * inv], axis=-1)
    cos_t = jnp.concatenate([jnp.cos(ang).T, jnp.ones((half, n_ctx), F32)], axis=1)
    sin_t = jnp.concatenate([jnp.sin(ang).T, jnp.zeros((half, n_ctx), F32)], axis=1)
    return cos_t, sin_t


def kernel(x, c, ctx, c_ctx, w_mod, b_mod, norm1, w_in, na_rpb, q_gain, k_gain, w_pa, w_pb, w_o,
           norm2, w_ffn_in, w_ffn_out, final_norm):
    b, s, d = x.shape
    n_ctx = ctx.shape[1]
    assert d == D_MODEL and n_ctx == CTX_LEN == TOK_TILE and s % TOK_TILE == 0
    assert s // GRID_W >= NA_WIN_H
    n_lat_tiles = s // TOK_TILE
    n_tiles = n_lat_tiles + 1

    perm = np.concatenate([np.arange(0, HEAD_DIM, 2), np.arange(1, HEAD_DIM, 2)])
    q_cols = (np.arange(GQA_Q_HEADS)[:, None] * HEAD_DIM + perm[None, :]).reshape(-1)
    k_cols = (np.arange(GQA_KV_HEADS)[:, None] * HEAD_DIM + perm[None, :]).reshape(-1)
    o_gq = 3 * NA_WIDTH
    o_gk = o_gq + GQA_Q_WIDTH
    o_gv = o_gk + GQA_KV_WIDTH
    o_ga = o_gv + GQA_KV_WIDTH

    cos_t, sin_t = _rope_tables(s, n_ctx)
    tok = jnp.concatenate([x, ctx], axis=1)

    n_rows_cc = 8
    cc = jnp.zeros((n_rows_cc, D_MODEL), F32).at[0:b].set(c).at[b].set(c_ctx)
    mod_all = _mod_call(cc, w_mod, b_mod)
    mod_lat = mod_all[:, 0:b].reshape(DEPTH, b, 1, 6, D_MODEL)
    mod_ctx = jnp.broadcast_to(mod_all[:, b].reshape(DEPTH, 1, 1, 6, D_MODEL),
                               (DEPTH, b, 1, 6, D_MODEL))
    mod_sel = jnp.concatenate([mod_lat, mod_ctx], axis=2)

    bias_slabs = _bias_call(na_rpb).reshape(DEPTH, NA_HEADS, N_DR - 1, GRID_W, 2 * GRID_W)

    out = None
    for l in range(DEPTH):
        last = l == DEPTH - 1
        w_l = w_in[l]
        wna = w_l[:, 0:o_gq].astype(BF16)
        wgt = jnp.concatenate(
            [w_l[:, o_gq:o_gk][:, q_cols], w_l[:, o_gk:o_gv][:, k_cols], w_l[:, o_gv:o_ga]],
            axis=1).T.astype(BF16)
        wg = w_l[:, o_ga:].astype(BF16)
        n1 = norm1[l].reshape(1, D_MODEL)
        n2 = norm2[l].reshape(1, D_MODEL)
        qg = q_gain[l][perm].reshape(HEAD_DIM, 1)
        kg = k_gain[l][perm].reshape(HEAD_DIM, 1)
        mod = mod_sel[l]

        na_qkv, gq, gk, gv = _inproj_call(tok, mod, n1, wna, wgt, qg, kg, cos_t, sin_t, n_lat_tiles)
        n_q_tiles = n_lat_tiles if last else n_tiles
        ya = _na_call(na_qkv, bias_slabs[l], n_q_tiles, n_lat_tiles)
        yb = _gqa_call(gq, gk, gv, n_q_tiles, n_lat_tiles)
        x1 = _merge_call(tok, mod, n1, ya, yb, wg, w_pa[l].astype(BF16), w_pb[l].astype(BF16),
                         w_o[l].astype(BF16), n_q_tiles, n_lat_tiles)
        tok_next = _ffn_call(x1, mod, n2, w_ffn_in[l].astype(BF16), w_ffn_out[l].astype(BF16),
                             final_norm.reshape(1, D_MODEL), last, n_lat_tiles)
        if last:
            out = tok_next
        else:
            tok = tok_next
    return out
```

```python
import functools
import math

import jax
import jax.numpy as jnp
import numpy as np
from jax import lax
from jax.experimental import pallas as pl
from jax.experimental.pallas import tpu as pltpu

D_MODEL = 1024
DEPTH = 4
GRID_W = 64
CTX_LEN = 256
HEAD_DIM = 64
NA_HEADS = 8
NA_WIN_H = 8
NA_WIN_W = 16
GQA_Q_HEADS = 8
GQA_KV_HEADS = 2
GQA_REP = GQA_Q_HEADS // GQA_KV_HEADS
NA_WIDTH = NA_HEADS * HEAD_DIM
GQA_Q_WIDTH = GQA_Q_HEADS * HEAD_DIM
GQA_KV_WIDTH = GQA_KV_HEADS * HEAD_DIM
D_FF = -(-(8 * D_MODEL) // (3 * 256)) * 256
ROPE_THETA = 10000.0
EPS = 1e-6
SCALE = HEAD_DIM ** -0.5
LOG2E = math.log2(math.e)
QK_SCALE = SCALE * LOG2E

TOK_TILE = 256
NA_ROWS_PER_TILE = TOK_TILE // GRID_W
NA_KEYS_WIN = NA_WIN_H * GRID_W
N_DR = 2 * NA_WIN_H - 1
N_DC = 2 * NA_WIN_W - 1
FF_CHUNK = 256
NA_PAIRS_PER_STEP = 2
K_EXT = 2 * HEAD_DIM
GQA_MAX_TILE_GROUP = 11
BOUND_SLACK = 1.01
MAX_SHIFT = 60.0
BF16_SUBLANES = 16
V_ROWS = HEAD_DIM + BF16_SUBLANES
VMEM_LIMIT_BYTES = 56 * 1024 * 1024

BF16 = jnp.bfloat16
F32 = jnp.float32


def _cparams(n_grid_dims):
    return pltpu.CompilerParams(
        dimension_semantics=("arbitrary",) * n_grid_dims,
        vmem_limit_bytes=VMEM_LIMIT_BYTES)


def _nt_dot(a, b):
    return lax.dot_general(a, b, (((1,), (1,)), ((), ())), preferred_element_type=F32)


def _dot(a, b):
    return jnp.dot(a, b, preferred_element_type=F32)


def _rms_mod(x, g, shift, scale):
    ms = jnp.mean(x * x, axis=-1, keepdims=True)
    y = x * lax.rsqrt(ms + EPS) * g
    return y * (1.0 + scale) + shift


def _mod_kernel(cc_ref, w_ref, b_ref, o_ref):
    cc = cc_ref[...]
    a = (cc * jax.nn.sigmoid(cc)).astype(BF16)
    o_ref[...] = _dot(a, w_ref[...].astype(BF16)) + b_ref[...]


def _mod_call(cc, w_mod, b_mod):
    n_rows = cc.shape[0]
    return pl.pallas_call(
        _mod_kernel,
        grid=(DEPTH, 6),
        in_specs=[
            pl.BlockSpec((n_rows, D_MODEL), lambda l, j: (0, 0)),
            pl.BlockSpec((None, D_MODEL, D_MODEL), lambda l, j: (l, 0, j)),
            pl.BlockSpec((None, 1, D_MODEL), lambda l, j: (l, 0, j)),
        ],
        out_specs=pl.BlockSpec((None, n_rows, D_MODEL), lambda l, j: (l, 0, j)),
        out_shape=jax.ShapeDtypeStruct((DEPTH, n_rows, 6 * D_MODEL), F32),
        compiler_params=_cparams(2),
        name="adaln_mod",
    )(cc, w_mod, b_mod.reshape(DEPTH, 1, 6 * D_MODEL))


def _bias_kernel(rpb_ref, o_ref):
    lh = pl.program_id(0)
    base = lh * (N_DR * N_DC)
    wq = lax.broadcasted_iota(jnp.int32, (GRID_W, 2 * GRID_W), 0)
    lane = lax.broadcasted_iota(jnp.int32, (GRID_W, 2 * GRID_W), 1)
    upper = lane >= GRID_W
    wk = jnp.bitwise_and(lane, GRID_W - 1)
    dci = jnp.clip(wk - wq, -(NA_WIN_W - 1), NA_WIN_W - 1) + (NA_WIN_W - 1)
    c_start = jnp.clip(wq - NA_WIN_W // 2, 0, GRID_W - NA_WIN_W)
    in_win = jnp.logical_and(wk >= c_start, wk < c_start + NA_WIN_W)

    def body(d, carry):
        acc = jnp.zeros((GRID_W, 2 * GRID_W), F32)
        for dc in range(N_DC):
            v_lo = rpb_ref[base + d * N_DC + dc]
            v_hi = rpb_ref[base + (d + 1) * N_DC + dc]
            acc = jnp.where(dci == dc, jnp.where(upper, v_hi, v_lo), acc)
        o_ref[d] = jnp.where(in_win, acc * LOG2E, -jnp.inf)
        return carry

    lax.fori_loop(0, N_DR - 1, body, 0)


def _bias_call(na_rpb):
    rpb_flat = na_rpb.astype(F32).reshape(-1)
    return pl.pallas_call(
        _bias_kernel,
        grid=(DEPTH * NA_HEADS,),
        in_specs=[pl.BlockSpec(memory_space=pltpu.SMEM)],
        out_specs=pl.BlockSpec((None, N_DR - 1, GRID_W, 2 * GRID_W), lambda i: (i, 0, 0, 0)),
        out_shape=jax.ShapeDtypeStruct((DEPTH * NA_HEADS, N_DR - 1, GRID_W, 2 * GRID_W), F32),
        compiler_params=_cparams(1),
        name="na_bias_slabs",
    )(rpb_flat)


def _inproj_kernel(tok_ref, mod_ref, n1_ref, wna_ref, wgt_ref, qg_ref, kg_ref, cos_ref, sin_ref,
                   na_ref, gq_ref, gk_ref, gv_ref, st_ref):
    x = tok_ref[...]
    h = _rms_mod(x, n1_ref[...], mod_ref[0:1, :], mod_ref[1:2, :]).astype(BF16)

    na = _dot(h, wna_ref[...])
    na_ref[:, 0:NA_WIDTH] = (na[:, 0:NA_WIDTH] * QK_SCALE).astype(BF16)
    na_ref[:, NA_WIDTH:3 * NA_WIDTH] = na[:, NA_WIDTH:3 * NA_WIDTH].astype(BF16)

    gt = _nt_dot(wgt_ref[...], h)
    cos = cos_ref[...]
    sin = sin_ref[...]
    half = HEAD_DIM // 2

    def norm_rope(xh, gain):
        ms = jnp.sum(xh * xh, axis=0, keepdims=True) * (1.0 / HEAD_DIM)
        y = xh * lax.rsqrt(ms + EPS) * gain
        y1 = y[0:half]
        y2 = y[half:HEAD_DIM]
        return jnp.concatenate([y1 * cos - y2 * sin, y1 * sin + y2 * cos], axis=0)

    def max_sq_norm(y):
        n2 = jnp.sum(y * y, axis=0, keepdims=True)
        return jnp.broadcast_to(jnp.max(n2, axis=1, keepdims=True), (1, 128))

    st_ref[...] = jnp.zeros(st_ref.shape, F32)
    qg = qg_ref[...]
    for hd in range(GQA_Q_HEADS):
        q = norm_rope(gt[hd * HEAD_DIM:(hd + 1) * HEAD_DIM], qg) * QK_SCALE
        gq_ref[hd] = q.astype(BF16)
        st_ref[hd:hd + 1, :] = max_sq_norm(q)
    kg = kg_ref[...]
    ones_col = (lax.broadcasted_iota(jnp.int32, (K_EXT - HEAD_DIM, TOK_TILE), 0) == 0).astype(F32)
    for kv in range(GQA_KV_HEADS):
        r0 = GQA_Q_WIDTH + kv * HEAD_DIM
        k = norm_rope(gt[r0:r0 + HEAD_DIM], kg)
        st_ref[GQA_Q_HEADS + kv:GQA_Q_HEADS + kv + 1, :] = max_sq_norm(k)
        gk_ref[kv] = jnp.concatenate([k, ones_col], axis=0).T.astype(BF16)
        r1 = GQA_Q_WIDTH + GQA_KV_WIDTH + kv * HEAD_DIM
        gv_ref[kv, 0:HEAD_DIM] = gt[r1:r1 + HEAD_DIM].astype(BF16)
        gv_ref[kv, HEAD_DIM:V_ROWS] = jnp.ones((V_ROWS - HEAD_DIM, TOK_TILE), BF16)


def _inproj_call(tok, mod, n1, wna, wgt, qg, kg, cos_t, sin_t, n_lat_tiles):
    b, n_tok, _ = tok.shape
    nt = n_tok // TOK_TILE
    return pl.pallas_call(
        _inproj_kernel,
        grid=(b, nt),
        in_specs=[
            pl.BlockSpec((None, TOK_TILE, D_MODEL), lambda bi, i: (bi, i, 0)),
            pl.BlockSpec((None, None, 6, D_MODEL), lambda bi, i: (bi, i // n_lat_tiles, 0, 0)),
            pl.BlockSpec((1, D_MODEL), lambda bi, i: (0, 0)),
            pl.BlockSpec((D_MODEL, 3 * NA_WIDTH), lambda bi, i: (0, 0)),
            pl.BlockSpec((GQA_Q_WIDTH + 2 * GQA_KV_WIDTH, D_MODEL), lambda bi, i: (0, 0)),
            pl.BlockSpec((HEAD_DIM, 1), lambda bi, i: (0, 0)),
            pl.BlockSpec((HEAD_DIM, 1), lambda bi, i: (0, 0)),
            pl.BlockSpec((HEAD_DIM // 2, TOK_TILE), lambda bi, i: (0, i)),
            pl.BlockSpec((HEAD_DIM // 2, TOK_TILE), lambda bi, i: (0, i)),
        ],
        out_specs=[
            pl.BlockSpec((None, TOK_TILE, 3 * NA_WIDTH), lambda bi, i: (bi, i, 0)),
            pl.BlockSpec((None, GQA_Q_HEADS, HEAD_DIM, TOK_TILE), lambda bi, i: (bi, 0, 0, i)),
            pl.BlockSpec((None, GQA_KV_HEADS, TOK_TILE, K_EXT), lambda bi, i: (bi, 0, i, 0)),
            pl.BlockSpec((None, GQA_KV_HEADS, None, V_ROWS, TOK_TILE),
                         lambda bi, i: (bi, 0, i, 0, 0)),
            pl.BlockSpec((None, None, 16, 128), lambda bi, i: (bi, i, 0, 0)),
        ],
        out_shape=[
            jax.ShapeDtypeStruct((b, n_tok, 3 * NA_WIDTH), BF16),
            jax.ShapeDtypeStruct((b, GQA_Q_HEADS, HEAD_DIM, n_tok), BF16),
            jax.ShapeDtypeStruct((b, GQA_KV_HEADS, n_tok, K_EXT), BF16),
            jax.ShapeDtypeStruct((b, GQA_KV_HEADS, nt, V_ROWS, TOK_TILE), BF16),
            jax.ShapeDtypeStruct((b, nt, 16, 128), F32),
        ],
        compiler_params=_cparams(2),
        name="in_proj",
    )(tok, mod, n1, wna, wgt, qg, kg, cos_t, sin_t)


def _na_kernel(q_ref, k_ref, v_ref, w_ref, o_ref, *, n_lat_tiles, n_rows):
    for pp in range(NA_PAIRS_PER_STEP):
        cols = slice(pp * 2 * HEAD_DIM, (pp + 1) * 2 * HEAD_DIM)
        _na_pair(q_ref.at[:, cols], k_ref.at[:, cols], v_ref.at[:, cols],
                 w_ref.at[2 * pp:2 * pp + 2], o_ref.at[:, cols],
                 n_lat_tiles=n_lat_tiles, n_rows=n_rows)


def _na_pair(q_ref, k_ref, v_ref, w_ref, o_ref, *, n_lat_tiles, n_rows):
    i = pl.program_id(2)
    n_lat = n_lat_tiles * TOK_TILE
    lane = lax.broadcasted_iota(jnp.int32, (1, 2 * HEAD_DIM), 1)
    lo_half = lane < HEAD_DIM
    k_ctx = k_ref[n_lat:n_lat + CTX_LEN, :]
    v_ctx = v_ref[n_lat:n_lat + CTX_LEN, :]

    q = q_ref[...]
    zero = jnp.zeros_like(q)
    q2 = jnp.concatenate([jnp.where(lo_half, q, zero), jnp.where(lo_half, zero, q)], axis=0)

    def both_heads(x, rr):
        lo = rr * GRID_W
        return jnp.concatenate([x[lo:lo + GRID_W], x[TOK_TILE + lo:TOK_TILE + lo + GRID_W]], axis=0)

    s_ctx = _nt_dot(q2, k_ctx)

    @pl.when(i < n_lat_tiles)
    def _latent():
        v_wins = []
        s_wins = []
        for rr in range(NA_ROWS_PER_TILE):
            r = i * NA_ROWS_PER_TILE + rr
            r_start = jnp.clip(r - NA_WIN_H // 2, 0, n_rows - NA_WIN_H)
            d0 = r_start - r + (NA_WIN_H - 1)
            k0 = pl.multiple_of(r_start * GRID_W, GRID_W)
            k_win = k_ref[pl.ds(k0, NA_KEYS_WIN), :]
            v_wins.append(v_ref[pl.ds(k0, NA_KEYS_WIN), :])
            bias = jnp.concatenate(
                [jnp.concatenate([w_ref[hh, d0 + 2 * jj] for jj in range(NA_WIN_H // 2)], axis=1)
                 for hh in range(2)], axis=0)
            s_wins.append(_nt_dot(both_heads(q2, rr), k_win) + bias)
        p_wins = []
        p_ctxs = []
        denoms = []
        for rr in range(NA_ROWS_PER_TILE):
            s_w = s_wins[rr]
            s_c = both_heads(s_ctx, rr)
            m = jnp.maximum(jnp.max(s_w, axis=-1, keepdims=True),
                            jnp.max(s_c, axis=-1, keepdims=True))
            p_w = jnp.exp2(s_w - m)
            p_c = jnp.exp2(s_c - m)
            denoms.append(jnp.sum(p_w, axis=-1, keepdims=True)
                          + jnp.sum(p_c, axis=-1, keepdims=True))
            p_wins.append(p_w.astype(BF16))
            p_ctxs.append(p_c.astype(BF16))
        p_ctx = jnp.concatenate([p[0:GRID_W] for p in p_ctxs] + [p[GRID_W:] for p in p_ctxs], axis=0)
        o_ctx = _dot(p_ctx, v_ctx)
        for rr in range(NA_ROWS_PER_TILE):
            o = (_dot(p_wins[rr], v_wins[rr]) + both_heads(o_ctx, rr)) / denoms[rr]
            o_ref[rr * GRID_W:(rr + 1) * GRID_W, :] = jnp.where(
                lo_half, o[0:GRID_W], o[GRID_W:]).astype(BF16)

    @pl.when(i >= n_lat_tiles)
    def _context():
        m = jnp.max(s_ctx, axis=-1, keepdims=True)
        p = jnp.exp2(s_ctx - m)
        o = _dot(p.astype(BF16), v_ctx) / jnp.sum(p, axis=-1, keepdims=True)
        o_ref[...] = jnp.where(lo_half, o[0:TOK_TILE], o[TOK_TILE:]).astype(BF16)


def _na_call(na_qkv, bias_slabs, n_q_tiles, n_lat_tiles):
    b, n_tok, _ = na_qkv.shape
    n_groups = NA_HEADS // (2 * NA_PAIRS_PER_STEP)
    width = 2 * HEAD_DIM * NA_PAIRS_PER_STEP
    kern = functools.partial(_na_kernel, n_lat_tiles=n_lat_tiles,
                             n_rows=n_lat_tiles * NA_ROWS_PER_TILE)
    return pl.pallas_call(
        kern,
        grid=(b, n_groups, n_q_tiles),
        in_specs=[
            pl.BlockSpec((None, TOK_TILE, width), lambda bi, hg, i: (bi, i, hg)),
            pl.BlockSpec((None, n_tok, width), lambda bi, hg, i: (bi, 0, n_groups + hg)),
            pl.BlockSpec((None, n_tok, width), lambda bi, hg, i: (bi, 0, 2 * n_groups + hg)),
            pl.BlockSpec((2 * NA_PAIRS_PER_STEP, N_DR - 1, GRID_W, 2 * GRID_W),
                         lambda bi, hg, i: (hg, 0, 0, 0)),
        ],
        out_specs=pl.BlockSpec((None, TOK_TILE, width), lambda bi, hg, i: (bi, i, hg)),
        out_shape=jax.ShapeDtypeStruct((b, n_q_tiles * TOK_TILE, NA_WIDTH), BF16),
        compiler_params=_cparams(3),
        name="na_attn",
    )(na_qkv, na_qkv, na_qkv, bias_slabs)


def _gqa_finish(o_ref, accs):
    outs = [a[0:HEAD_DIM] / a[HEAD_DIM:HEAD_DIM + 1] for a in accs]
    o_ref[...] = jnp.concatenate(outs, axis=0).T.astype(BF16)


def _gqa_bounded_kernel(q_ref, k_ref, v_ref, sh_ref, o_ref, qx_sc, p_sc, acc_sc, *,
                        n_lat_tiles, group):
    i = pl.program_id(2)
    n_key_tiles = n_lat_tiles + 1
    row = lax.broadcasted_iota(jnp.int32, (K_EXT - HEAD_DIM, TOK_TILE), 0)
    for hd in range(GQA_REP):
        qx_sc[hd, 0:HEAD_DIM] = q_ref[hd]
        qx_sc[hd, HEAD_DIM:K_EXT] = jnp.where(row == 0, -sh_ref[hd][:, 0:1], 0.0).astype(BF16)

    def key_tile(c):
        return k_ref[pl.ds(pl.multiple_of(c * TOK_TILE, TOK_TILE), TOK_TILE), :]

    def probs(k_c, hd):
        return jnp.exp2(_dot(k_c, qx_sc[hd])).astype(BF16)

    @pl.when(i < n_lat_tiles)
    def _latent():
        acc_sc[...] = jnp.zeros(acc_sc.shape, F32)
        p_sc[...] = jnp.zeros(p_sc.shape, BF16)

        def tiles(gi, carry):
            c0 = gi * group
            p_prev = [p_sc[hd] for hd in range(GQA_REP)]
            c_prev = jnp.maximum(c0 - 1, 0)
            for cc in range(group):
                k_c = key_tile(c0 + cc)
                v_p = v_ref[c_prev]
                p_new = []
                for hd in range(GQA_REP):
                    p_new.append(probs(k_c, hd))
                    acc_sc[hd] = acc_sc[hd] + _dot(v_p, p_prev[hd])
                p_prev = p_new
                c_prev = c0 + cc
            for hd in range(GQA_REP):
                p_sc[hd] = p_prev[hd]
            return carry

        lax.fori_loop(0, n_key_tiles // group, tiles, 0)
        v_p = v_ref[n_key_tiles - 1]
        _gqa_finish(o_ref, [acc_sc[hd] + _dot(v_p, p_sc[hd]) for hd in range(GQA_REP)])

    @pl.when(i >= n_lat_tiles)
    def _context():
        k_c = key_tile(n_lat_tiles)
        v_c = v_ref[n_lat_tiles]
        _gqa_finish(o_ref, [_dot(v_c, probs(k_c, hd)) for hd in range(GQA_REP)])


def _gqa_kernel(q_ref, k_ref, v_ref, o_ref, s_a, s_b, cm_a, cm_b, p_a, p_b, m_sc, acc_sc, *,
                n_lat_tiles):
    i = pl.program_id(2)

    def key_tile(c):
        return k_ref[pl.ds(pl.multiple_of(c * TOK_TILE, TOK_TILE), TOK_TILE), 0:HEAD_DIM]

    def finish(accs):
        _gqa_finish(o_ref, accs)

    def step(c_prev, c_next, s_cur, cm_cur, s_nxt, cm_nxt, p_prev, p_cur):
        v_prev = v_ref[c_prev]
        k_next = key_tile(c_next)
        for hd in range(GQA_REP):
            pv = _dot(v_prev, p_prev[hd])
            s = _dot(k_next, q_ref[hd])
            s_nxt[hd] = s
            cm_nxt[hd] = jnp.max(s, axis=0, keepdims=True)
            m_old = m_sc[hd]
            m_new = jnp.maximum(m_old, cm_cur[hd])
            alpha = jnp.exp2(m_old - m_new)
            p_cur[hd] = jnp.exp2(s_cur[hd] - m_new).astype(BF16)
            acc_sc[hd] = (acc_sc[hd] + pv) * alpha
            m_sc[hd] = m_new

    @pl.when(i < n_lat_tiles)
    def _latent():
        m_sc[...] = jnp.full(m_sc.shape, -jnp.inf, F32)
        acc_sc[...] = jnp.zeros(acc_sc.shape, F32)
        p_b[...] = jnp.zeros(p_b.shape, BF16)
        k_0 = key_tile(0)
        for hd in range(GQA_REP):
            s = _dot(k_0, q_ref[hd])
            s_a[hd] = s
            cm_a[hd] = jnp.max(s, axis=0, keepdims=True)

        def pair(j, carry):
            c = 2 * j
            step(jnp.maximum(c - 1, 0), c + 1, s_a, cm_a, s_b, cm_b, p_b, p_a)
            step(c, c + 2, s_b, cm_b, s_a, cm_a, p_a, p_b)
            return carry

        lax.fori_loop(0, n_lat_tiles // 2, pair, 0, unroll=8)
        v_last = v_ref[n_lat_tiles - 1]
        v_ctx = v_ref[n_lat_tiles]
        accs = []
        for hd in range(GQA_REP):
            pv = _dot(v_last, p_b[hd])
            m_old = m_sc[hd]
            m_new = jnp.maximum(m_old, cm_a[hd])
            alpha = jnp.exp2(m_old - m_new)
            p = jnp.exp2(s_a[hd] - m_new).astype(BF16)
            accs.append((acc_sc[hd] + pv) * alpha + _dot(v_ctx, p))
        finish(accs)

    @pl.when(i >= n_lat_tiles)
    def _context():
        k_c = key_tile(n_lat_tiles)
        v_c = v_ref[n_lat_tiles]
        accs = []
        for hd in range(GQA_REP):
            s = _dot(k_c, q_ref[hd])
            p = jnp.exp2(s - jnp.max(s, axis=0, keepdims=True)).astype(BF16)
            accs.append(_dot(v_c, p))
        finish(accs)


def _gqa_specs(gq, gv, n_q_tiles):
    b = gq.shape[0]
    n_tok = gq.shape[-1]
    n_key_tiles = gv.shape[2]
    in_specs = [
        pl.BlockSpec((None, GQA_REP, HEAD_DIM, TOK_TILE), lambda bi, g, i: (bi, g, 0, i)),
        pl.BlockSpec((None, None, n_tok, K_EXT), lambda bi, g, i: (bi, g, 0, 0)),
        pl.BlockSpec((None, None, n_key_tiles, V_ROWS, TOK_TILE), lambda bi, g, i: (bi, g, 0, 0, 0)),
    ]
    out_spec = pl.BlockSpec((None, TOK_TILE, GQA_REP * HEAD_DIM), lambda bi, g, i: (bi, i, g))
    out_shape = jax.ShapeDtypeStruct((b, n_q_tiles * TOK_TILE, GQA_Q_WIDTH), BF16)
    return (b, GQA_KV_HEADS, n_q_tiles), in_specs, out_spec, out_shape


def _gqa_online_call(gq, gk, gv, n_q_tiles, n_lat_tiles):
    assert n_lat_tiles % 2 == 0 and gv.shape[2] == n_lat_tiles + 1
    grid, in_specs, out_spec, out_shape = _gqa_specs(gq, gv, n_q_tiles)
    score_buf = pltpu.VMEM((GQA_REP, TOK_TILE, TOK_TILE), F32)
    stat_buf = pltpu.VMEM((GQA_REP, 1, TOK_TILE), F32)
    prob_buf = pltpu.VMEM((GQA_REP, TOK_TILE, TOK_TILE), BF16)
    return pl.pallas_call(
        functools.partial(_gqa_kernel, n_lat_tiles=n_lat_tiles),
        grid=grid, in_specs=in_specs, out_specs=out_spec, out_shape=out_shape,
        scratch_shapes=[score_buf, score_buf, stat_buf, stat_buf, prob_buf, prob_buf, stat_buf,
                        pltpu.VMEM((GQA_REP, V_ROWS, TOK_TILE), F32)],
        compiler_params=_cparams(3),
        name="gqa_attn_online",
    )(gq, gk, gv)


def _gqa_bounded_call(gq, gk, gv, shift, n_q_tiles, n_lat_tiles):
    n_key_tiles = gv.shape[2]
    assert n_key_tiles == n_lat_tiles + 1
    group = max(g for g in range(1, GQA_MAX_TILE_GROUP + 1) if n_key_tiles % g == 0)
    grid, in_specs, out_spec, out_shape = _gqa_specs(gq, gv, n_q_tiles)
    in_specs.append(pl.BlockSpec((None, GQA_REP, 1, 128), lambda bi, g, i: (bi, g, 0, 0)))
    return pl.pallas_call(
        functools.partial(_gqa_bounded_kernel, n_lat_tiles=n_lat_tiles, group=group),
        grid=grid, in_specs=in_specs, out_specs=out_spec, out_shape=out_shape,
        scratch_shapes=[pltpu.VMEM((GQA_REP, K_EXT, TOK_TILE), BF16),
                        pltpu.VMEM((GQA_REP, TOK_TILE, TOK_TILE), BF16),
                        pltpu.VMEM((GQA_REP, V_ROWS, TOK_TILE), F32)],
        compiler_params=_cparams(3),
        name="gqa_attn_bounded",
    )(gq, gk, gv, shift)


def _gqa_call(gq, gk, gv, stats, n_q_tiles, n_lat_tiles):
    n2 = jnp.max(stats[:, :, :, 0], axis=1)
    q_max = jnp.sqrt(n2[:, 0:GQA_Q_HEADS])
    k_max = jnp.repeat(jnp.sqrt(n2[:, GQA_Q_HEADS:GQA_Q_HEADS + GQA_KV_HEADS]), GQA_REP, axis=1)
    bound = q_max * k_max * BOUND_SLACK
    shift = jnp.broadcast_to(bound[:, :, None, None], bound.shape + (1, 128))
    return lax.cond(
        jnp.max(bound) <= MAX_SHIFT,
        lambda: _gqa_bounded_call(gq, gk, gv, shift, n_q_tiles, n_lat_tiles),
        lambda: _gqa_online_call(gq, gk, gv, n_q_tiles, n_lat_tiles))


def _mix_ffn_kernel(tok_ref, mod_ref, n1_ref, n2_ref, ya_ref, yb_ref, wg_ref, wpa_ref, wpb_ref,
                    wo_ref, win_ref, wout_ref, fn_ref, o_ref, g_sc, *, final_norm):
    x = tok_ref[...]
    h = _rms_mod(x, n1_ref[...], mod_ref[0:1, :], mod_ref[1:2, :]).astype(BF16)
    gates = _dot(h, wg_ref[...])
    pa = _dot(ya_ref[...], wpa_ref[...])
    pb = _dot(yb_ref[...], wpb_ref[...])
    merged = (jax.nn.sigmoid(gates[:, 0:D_MODEL]) * pa
              + jax.nn.sigmoid(gates[:, D_MODEL:2 * D_MODEL]) * pb)
    x = x + mod_ref[2:3, :] * _dot(merged.astype(BF16), wo_ref[...])
    h = _rms_mod(x, n2_ref[...], mod_ref[3:4, :], mod_ref[4:5, :]).astype(BF16)
    for j in range(D_FF // FF_CHUNK):
        c0 = j * FF_CHUNK
        a = _dot(h, win_ref[:, c0:c0 + FF_CHUNK])
        u = _dot(h, win_ref[:, D_FF + c0:D_FF + c0 + FF_CHUNK])
        g_sc[:, c0:c0 + FF_CHUNK] = (a * jax.nn.sigmoid(a) * u).astype(BF16)
    y = x + mod_ref[5:6, :] * _dot(g_sc[...], wout_ref[...])
    if final_norm:
        ms = jnp.mean(y * y, axis=-1, keepdims=True)
        y = y * lax.rsqrt(ms + EPS) * fn_ref[...]
    o_ref[...] = y


def _mix_ffn_call(tok, mod, n1, n2, ya, yb, wg, wpa, wpb, wo, win, wout, fn, final_norm, n_tiles,
                  n_lat_tiles):
    b = tok.shape[0]
    const = lambda shp: pl.BlockSpec(shp, lambda bi, i: (0,) * len(shp),
                                     pipeline_mode=pl.Buffered(1))
    kern = functools.partial(_mix_ffn_kernel, final_norm=final_norm)
    return pl.pallas_call(
        kern,
        grid=(b, n_tiles),
        in_specs=[
            pl.BlockSpec((None, TOK_TILE, D_MODEL), lambda bi, i: (bi, i, 0)),
            pl.BlockSpec((None, None, 6, D_MODEL), lambda bi, i: (bi, i // n_lat_tiles, 0, 0)),
            const((1, D_MODEL)),
            const((1, D_MODEL)),
            pl.BlockSpec((None, TOK_TILE, NA_WIDTH), lambda bi, i: (bi, i, 0)),
            pl.BlockSpec((None, TOK_TILE, GQA_Q_WIDTH), lambda bi, i: (bi, i, 0)),
            const((D_MODEL, 2 * D_MODEL)),
            const((NA_WIDTH, D_MODEL)),
            const((GQA_Q_WIDTH, D_MODEL)),
            const((D_MODEL, D_MODEL)),
            const((D_MODEL, 2 * D_FF)),
            const((D_FF, D_MODEL)),
            const((1, D_MODEL)),
        ],
        out_specs=pl.BlockSpec((None, TOK_TILE, D_MODEL), lambda bi, i: (bi, i, 0)),
        out_shape=jax.ShapeDtypeStruct((b, n_tiles * TOK_TILE, D_MODEL), F32),
        scratch_shapes=[pltpu.VMEM((TOK_TILE, D_FF), BF16)],
        compiler_params=_cparams(2),
        name="mix_ffn",
    )(tok, mod, n1, n2, ya, yb, wg, wpa, wpb, wo, win, wout, fn)


def _rope_tables(n_lat, n_ctx):
    t = jnp.arange(n_lat)
    row = (t // GRID_W).astype(F32)
    col = (t % GRID_W).astype(F32)
    half = HEAD_DIM // 2
    inv = ROPE_THETA ** (-jnp.arange(0, half, 2, dtype=F32) / half)
    ang = jnp.concatenate([row[:, None] * inv, col[:, None] * inv], axis=-1)
    cos_t = jnp.concatenate([jnp.cos(ang).T, jnp.ones((half, n_ctx), F32)], axis=1)
    sin_t = jnp.concatenate([jnp.sin(ang).T, jnp.zeros((half, n_ctx), F32)], axis=1)
    return cos_t, sin_t


def kernel(x, c, ctx, c_ctx, w_mod, b_mod, norm1, w_in, na_rpb, q_gain, k_gain, w_pa, w_pb, w_o,
           norm2, w_ffn_in, w_ffn_out, final_norm):
    b, s, d = x.shape
    n_ctx = ctx.shape[1]
    assert d == D_MODEL and n_ctx == CTX_LEN == TOK_TILE and s % TOK_TILE == 0
    assert s // GRID_W >= NA_WIN_H
    n_lat_tiles = s // TOK_TILE
    n_tiles = n_lat_tiles + 1

    perm = np.concatenate([np.arange(0, HEAD_DIM, 2), np.arange(1, HEAD_DIM, 2)])
    q_cols = (np.arange(GQA_Q_HEADS)[:, None] * HEAD_DIM + perm[None, :]).reshape(-1)
    k_cols = (np.arange(GQA_KV_HEADS)[:, None] * HEAD_DIM + perm[None, :]).reshape(-1)
    o_gq = 3 * NA_WIDTH
    o_gk = o_gq + GQA_Q_WIDTH
    o_gv = o_gk + GQA_KV_WIDTH
    o_ga = o_gv + GQA_KV_WIDTH

    cos_t, sin_t = _rope_tables(s, n_ctx)
    tok = jnp.concatenate([x, ctx], axis=1)

    n_rows_cc = 8
    cc = jnp.zeros((n_rows_cc, D_MODEL), F32).at[0:b].set(c).at[b].set(c_ctx)
    mod_all = _mod_call(cc, w_mod, b_mod)
    mod_lat = mod_all[:, 0:b].reshape(DEPTH, b, 1, 6, D_MODEL)
    mod_ctx = jnp.broadcast_to(mod_all[:, b].reshape(DEPTH, 1, 1, 6, D_MODEL),
                               (DEPTH, b, 1, 6, D_MODEL))
    mod_sel = jnp.concatenate([mod_lat, mod_ctx], axis=2)

    bias_slabs = _bias_call(na_rpb).reshape(DEPTH, NA_HEADS, N_DR - 1, GRID_W, 2 * GRID_W)

    out = None
    for l in range(DEPTH):
        last = l == DEPTH - 1
        w_l = w_in[l]
        wna = w_l[:, 0:o_gq].astype(BF16)
        wgt = jnp.concatenate(
            [w_l[:, o_gq:o_gk][:, q_cols], w_l[:, o_gk:o_gv][:, k_cols], w_l[:, o_gv:o_ga]],
            axis=1).T.astype(BF16)
        wg = w_l[:, o_ga:].astype(BF16)
        n1 = norm1[l].reshape(1, D_MODEL)
        n2 = norm2[l].reshape(1, D_MODEL)
        qg = q_gain[l][perm].reshape(HEAD_DIM, 1)
        kg = k_gain[l][perm].reshape(HEAD_DIM, 1)
        mod = mod_sel[l]

        na_qkv, gq, gk, gv, stats = _inproj_call(tok, mod, n1, wna, wgt, qg, kg, cos_t, sin_t,
                                                 n_lat_tiles)
        n_q_tiles = n_lat_tiles if last else n_tiles
        ya = _na_call(na_qkv, bias_slabs[l], n_q_tiles, n_lat_tiles)
        yb = _gqa_call(gq, gk, gv, stats, n_q_tiles, n_lat_tiles)
        tok_next = _mix_ffn_call(tok, mod, n1, n2, ya, yb, wg, w_pa[l].astype(BF16),
                                 w_pb[l].astype(BF16), w_o[l].astype(BF16),
                                 w_ffn_in[l].astype(BF16), w_ffn_out[l].astype(BF16),
                                 final_norm.reshape(1, D_MODEL), last, n_q_tiles, n_lat_tiles)
        if last:
            out = tok_next
        else:
            tok = tok_next
    return out
```

```python
import functools
import math

import jax
import jax.numpy as jnp
import numpy as np
from jax import lax
from jax.experimental import pallas as pl
from jax.experimental.pallas import tpu as pltpu

D_MODEL = 1024
DEPTH = 4
GRID_W = 64
CTX_LEN = 256
HEAD_DIM = 64
NA_HEADS = 8
NA_WIN_H = 8
NA_WIN_W = 16
GQA_Q_HEADS = 8
GQA_KV_HEADS = 2
GQA_REP = GQA_Q_HEADS // GQA_KV_HEADS
NA_WIDTH = NA_HEADS * HEAD_DIM
GQA_Q_WIDTH = GQA_Q_HEADS * HEAD_DIM
GQA_KV_WIDTH = GQA_KV_HEADS * HEAD_DIM
D_FF = -(-(8 * D_MODEL) // (3 * 256)) * 256
ROPE_THETA = 10000.0
EPS = 1e-6
SCALE = HEAD_DIM ** -0.5
LOG2E = math.log2(math.e)
QK_SCALE = SCALE * LOG2E

TOK_TILE = 256
NA_ROWS_PER_TILE = TOK_TILE // GRID_W
NA_KEYS_WIN = NA_WIN_H * GRID_W
N_DR = 2 * NA_WIN_H - 1
N_DC = 2 * NA_WIN_W - 1
FF_CHUNK = 256
NA_PAIRS_PER_STEP = 2
K_EXT = 2 * HEAD_DIM
GQA_MAX_TILE_GROUP = 11
BOUND_SLACK = 1.01
MAX_SHIFT = 60.0
F32_SUBLANES = 8
BF16_SUBLANES = 16
V_ROWS = HEAD_DIM + BF16_SUBLANES
VMEM_LIMIT_BYTES = 56 * 1024 * 1024

BF16 = jnp.bfloat16
F32 = jnp.float32


def _cparams(n_grid_dims):
    return pltpu.CompilerParams(
        dimension_semantics=("arbitrary",) * n_grid_dims,
        vmem_limit_bytes=VMEM_LIMIT_BYTES)


def _nt_dot(a, b):
    return lax.dot_general(a, b, (((1,), (1,)), ((), ())), preferred_element_type=F32)


def _dot(a, b):
    return jnp.dot(a, b, preferred_element_type=F32)


def _rms_mod(x, g, shift, scale):
    ms = jnp.mean(x * x, axis=-1, keepdims=True)
    y = x * lax.rsqrt(ms + EPS) * g
    return y * (1.0 + scale) + shift


def _mod_kernel(cc_ref, w_ref, b_ref, o_ref):
    cc = cc_ref[...]
    a = (cc * jax.nn.sigmoid(cc)).astype(BF16)
    o_ref[...] = _dot(a, w_ref[...].astype(BF16)) + b_ref[...]


def _mod_call(cc, w_mod, b_mod):
    n_rows = cc.shape[0]
    return pl.pallas_call(
        _mod_kernel,
        grid=(DEPTH, 6),
        in_specs=[
            pl.BlockSpec((n_rows, D_MODEL), lambda l, j: (0, 0)),
            pl.BlockSpec((None, D_MODEL, D_MODEL), lambda l, j: (l, 0, j)),
            pl.BlockSpec((None, 1, D_MODEL), lambda l, j: (l, 0, j)),
        ],
        out_specs=pl.BlockSpec((None, n_rows, D_MODEL), lambda l, j: (l, 0, j)),
        out_shape=jax.ShapeDtypeStruct((DEPTH, n_rows, 6 * D_MODEL), F32),
        compiler_params=_cparams(2),
        name="adaln_mod",
    )(cc, w_mod, b_mod.reshape(DEPTH, 1, 6 * D_MODEL))


def _bias_kernel(rpb_ref, o_ref):
    lh = pl.program_id(0)
    base = lh * (N_DR * N_DC)
    wq = lax.broadcasted_iota(jnp.int32, (GRID_W, 2 * GRID_W), 0)
    lane = lax.broadcasted_iota(jnp.int32, (GRID_W, 2 * GRID_W), 1)
    upper = lane >= GRID_W
    wk = jnp.bitwise_and(lane, GRID_W - 1)
    dci = jnp.clip(wk - wq, -(NA_WIN_W - 1), NA_WIN_W - 1) + (NA_WIN_W - 1)
    c_start = jnp.clip(wq - NA_WIN_W // 2, 0, GRID_W - NA_WIN_W)
    in_win = jnp.logical_and(wk >= c_start, wk < c_start + NA_WIN_W)

    def body(d, carry):
        acc = jnp.zeros((GRID_W, 2 * GRID_W), F32)
        for dc in range(N_DC):
            v_lo = rpb_ref[base + d * N_DC + dc]
            v_hi = rpb_ref[base + (d + 1) * N_DC + dc]
            acc = jnp.where(dci == dc, jnp.where(upper, v_hi, v_lo), acc)
        o_ref[d] = jnp.where(in_win, acc * LOG2E, -jnp.inf)
        return carry

    lax.fori_loop(0, N_DR - 1, body, 0)


def _bias_call(na_rpb):
    rpb_flat = na_rpb.astype(F32).reshape(-1)
    return pl.pallas_call(
        _bias_kernel,
        grid=(DEPTH * NA_HEADS,),
        in_specs=[pl.BlockSpec(memory_space=pltpu.SMEM)],
        out_specs=pl.BlockSpec((None, N_DR - 1, GRID_W, 2 * GRID_W), lambda i: (i, 0, 0, 0)),
        out_shape=jax.ShapeDtypeStruct((DEPTH * NA_HEADS, N_DR - 1, GRID_W, 2 * GRID_W), F32),
        compiler_params=_cparams(1),
        name="na_bias_slabs",
    )(rpb_flat)


def _inproj_kernel(tok_ref, mod_ref, n1_ref, wna_ref, wgt_ref, qg_ref, kg_ref, cos_ref, sin_ref,
                   na_ref, gq_ref, gk_ref, gv_ref, st_ref):
    _inproj_body(tok_ref[...], mod_ref, n1_ref, wna_ref, wgt_ref, qg_ref, kg_ref, cos_ref, sin_ref,
                 na_ref, gq_ref, gk_ref, gv_ref, st_ref)


def _inproj_body(x, mod_ref, n1_ref, wna_ref, wgt_ref, qg_ref, kg_ref, cos_ref, sin_ref,
                 na_ref, gq_ref, gk_ref, gv_ref, st_ref):
    h = _rms_mod(x, n1_ref[...], mod_ref[0:1, :], mod_ref[1:2, :]).astype(BF16)

    na = _dot(h, wna_ref[...])
    na_ref[:, 0:NA_WIDTH] = (na[:, 0:NA_WIDTH] * QK_SCALE).astype(BF16)
    na_ref[:, NA_WIDTH:3 * NA_WIDTH] = na[:, NA_WIDTH:3 * NA_WIDTH].astype(BF16)

    gt = _nt_dot(wgt_ref[...], h)
    cos = cos_ref[...]
    sin = sin_ref[...]
    half = HEAD_DIM // 2

    def norm_rope(xh, gain):
        ms = jnp.sum(xh * xh, axis=0, keepdims=True) * (1.0 / HEAD_DIM)
        y = xh * lax.rsqrt(ms + EPS) * gain
        y1 = y[0:half]
        y2 = y[half:HEAD_DIM]
        return jnp.concatenate([y1 * cos - y2 * sin, y1 * sin + y2 * cos], axis=0)

    def max_sq_norm(y):
        n2 = jnp.sum(y * y, axis=0, keepdims=True)
        return jnp.broadcast_to(jnp.max(n2, axis=1, keepdims=True), (1, 128))

    st_ref[...] = jnp.zeros(st_ref.shape, F32)
    qg = qg_ref[...]
    for hd in range(GQA_Q_HEADS):
        q = norm_rope(gt[hd * HEAD_DIM:(hd + 1) * HEAD_DIM], qg) * QK_SCALE
        gq_ref[hd] = q.astype(BF16)
        st_ref[hd:hd + 1, :] = max_sq_norm(q)
    kg = kg_ref[...]
    ones_col = (lax.broadcasted_iota(jnp.int32, (K_EXT - HEAD_DIM, TOK_TILE), 0) == 0).astype(F32)
    for kv in range(GQA_KV_HEADS):
        r0 = GQA_Q_WIDTH + kv * HEAD_DIM
        k = norm_rope(gt[r0:r0 + HEAD_DIM], kg)
        st_ref[GQA_Q_HEADS + kv:GQA_Q_HEADS + kv + 1, :] = max_sq_norm(k)
        gk_ref[kv] = jnp.concatenate([k, ones_col], axis=0).T.astype(BF16)
        r1 = GQA_Q_WIDTH + GQA_KV_WIDTH + kv * HEAD_DIM
        gv_ref[kv, 0:HEAD_DIM] = gt[r1:r1 + HEAD_DIM].astype(BF16)
        gv_ref[kv, HEAD_DIM:V_ROWS] = jnp.ones((V_ROWS - HEAD_DIM, TOK_TILE), BF16)


def _const_spec(shp):
    return pl.BlockSpec(shp, lambda bi, i: (0,) * len(shp), pipeline_mode=pl.Buffered(1))


def _mod_spec(n_lat_tiles):
    return pl.BlockSpec((None, None, 6, D_MODEL), lambda bi, i: (bi, i // n_lat_tiles, 0, 0))


def _inproj_param_specs(n_lat_tiles):
    return [
        _mod_spec(n_lat_tiles),
        _const_spec((1, D_MODEL)),
        _const_spec((D_MODEL, 3 * NA_WIDTH)),
        _const_spec((GQA_Q_WIDTH + 2 * GQA_KV_WIDTH, D_MODEL)),
        _const_spec((HEAD_DIM, 1)),
        _const_spec((HEAD_DIM, 1)),
        pl.BlockSpec((HEAD_DIM // 2, TOK_TILE), lambda bi, i: (0, i)),
        pl.BlockSpec((HEAD_DIM // 2, TOK_TILE), lambda bi, i: (0, i)),
    ]


def _inproj_out_specs(b, nt):
    n_tok = nt * TOK_TILE
    specs = [
        pl.BlockSpec((None, TOK_TILE, 3 * NA_WIDTH), lambda bi, i: (bi, i, 0)),
        pl.BlockSpec((None, GQA_Q_HEADS, HEAD_DIM, TOK_TILE), lambda bi, i: (bi, 0, 0, i)),
        pl.BlockSpec((None, GQA_KV_HEADS, TOK_TILE, K_EXT), lambda bi, i: (bi, 0, i, 0)),
        pl.BlockSpec((None, GQA_KV_HEADS, None, V_ROWS, TOK_TILE), lambda bi, i: (bi, 0, i, 0, 0)),
        pl.BlockSpec((None, None, 16, 128), lambda bi, i: (bi, i, 0, 0)),
    ]
    shapes = [
        jax.ShapeDtypeStruct((b, n_tok, 3 * NA_WIDTH), BF16),
        jax.ShapeDtypeStruct((b, GQA_Q_HEADS, HEAD_DIM, n_tok), BF16),
        jax.ShapeDtypeStruct((b, GQA_KV_HEADS, n_tok, K_EXT), BF16),
        jax.ShapeDtypeStruct((b, GQA_KV_HEADS, nt, V_ROWS, TOK_TILE), BF16),
        jax.ShapeDtypeStruct((b, nt, 16, 128), F32),
    ]
    return specs, shapes


def _inproj_call(tok, params, n_lat_tiles):
    b, n_tok, _ = tok.shape
    nt = n_tok // TOK_TILE
    out_specs, out_shape = _inproj_out_specs(b, nt)
    return pl.pallas_call(
        _inproj_kernel,
        grid=(b, nt),
        in_specs=[pl.BlockSpec((None, TOK_TILE, D_MODEL), lambda bi, i: (bi, i, 0))]
        + _inproj_param_specs(n_lat_tiles),
        out_specs=out_specs,
        out_shape=out_shape,
        compiler_params=_cparams(2),
        name="in_proj",
    )(tok, *params)


def _na_kernel(q_ref, k_ref, v_ref, w_ref, o_ref, *, n_lat_tiles, n_rows):
    i = pl.program_id(2)
    n_lat = n_lat_tiles * TOK_TILE
    pair_w = 2 * HEAD_DIM
    lane = lax.broadcasted_iota(jnp.int32, (1, pair_w), 1)
    lo_half = lane < HEAD_DIM
    pairs = range(NA_PAIRS_PER_STEP)
    cols = [slice(pp * pair_w, (pp + 1) * pair_w) for pp in pairs]

    def with_ones(v):
        return jnp.concatenate([v, jnp.ones(v.shape, BF16)], axis=1)

    def both_heads(x, rr):
        lo = rr * GRID_W
        return jnp.concatenate([x[lo:lo + GRID_W], x[TOK_TILE + lo:TOK_TILE + lo + GRID_W]], axis=0)

    def normalise(o):
        return o[:, 0:pair_w] / o[:, pair_w:2 * pair_w]

    q2s = []
    s_ctxs = []
    v_ctxs = []
    for pp in pairs:
        q = q_ref[:, cols[pp]]
        zero = jnp.zeros_like(q)
        q2 = jnp.concatenate([jnp.where(lo_half, q, zero), jnp.where(lo_half, zero, q)], axis=0)
        q2s.append(q2)
        s_ctxs.append(_nt_dot(q2, k_ref[n_lat:n_lat + CTX_LEN, cols[pp]]))
        v_ctxs.append(with_ones(v_ref[n_lat:n_lat + CTX_LEN, cols[pp]]))

    @pl.when(i < n_lat_tiles)
    def _latent():
        units = [(pp, rr) for pp in pairs for rr in range(NA_ROWS_PER_TILE)]
        k0s = []
        d0s = []
        for rr in range(NA_ROWS_PER_TILE):
            r = i * NA_ROWS_PER_TILE + rr
            r_start = jnp.clip(r - NA_WIN_H // 2, 0, n_rows - NA_WIN_H)
            d0s.append(r_start - r + (NA_WIN_H - 1))
            k0s.append(pl.multiple_of(r_start * GRID_W, GRID_W))
        s_wins = {}
        for pp, rr in units:
            k_win = k_ref[pl.ds(k0s[rr], NA_KEYS_WIN), cols[pp]]
            bias = jnp.concatenate(
                [jnp.concatenate([w_ref[2 * pp + hh, d0s[rr] + 2 * jj]
                                  for jj in range(NA_WIN_H // 2)], axis=1)
                 for hh in range(2)], axis=0)
            s_wins[pp, rr] = _nt_dot(both_heads(q2s[pp], rr), k_win) + bias
        p_wins = {}
        p_ctxs = {}
        for pp, rr in units:
            s_w = s_wins[pp, rr]
            s_c = both_heads(s_ctxs[pp], rr)
            m = jnp.maximum(jnp.max(s_w, axis=-1, keepdims=True),
                            jnp.max(s_c, axis=-1, keepdims=True))
            p_wins[pp, rr] = jnp.exp2(s_w - m).astype(BF16)
            p_ctxs[pp, rr] = jnp.exp2(s_c - m).astype(BF16)
        for pp in pairs:
            rows = range(NA_ROWS_PER_TILE)
            p_ctx = jnp.concatenate([p_ctxs[pp, rr][0:GRID_W] for rr in rows]
                                    + [p_ctxs[pp, rr][GRID_W:] for rr in rows], axis=0)
            o_ctx = _dot(p_ctx, v_ctxs[pp])
            for rr in rows:
                v_win = with_ones(v_ref[pl.ds(k0s[rr], NA_KEYS_WIN), cols[pp]])
                o = normalise(_dot(p_wins[pp, rr], v_win) + both_heads(o_ctx, rr))
                o_ref[rr * GRID_W:(rr + 1) * GRID_W, cols[pp]] = jnp.where(
                    lo_half, o[0:GRID_W], o[GRID_W:]).astype(BF16)

    @pl.when(i >= n_lat_tiles)
    def _context():
        for pp in pairs:
            s = s_ctxs[pp]
            p = jnp.exp2(s - jnp.max(s, axis=-1, keepdims=True))
            o = normalise(_dot(p.astype(BF16), v_ctxs[pp]))
            o_ref[:, cols[pp]] = jnp.where(lo_half, o[0:TOK_TILE], o[TOK_TILE:]).astype(BF16)


def _na_call(na_qkv, bias_slabs, n_q_tiles, n_lat_tiles):
    b, n_tok, _ = na_qkv.shape
    n_groups = NA_HEADS // (2 * NA_PAIRS_PER_STEP)
    width = 2 * HEAD_DIM * NA_PAIRS_PER_STEP
    kern = functools.partial(_na_kernel, n_lat_tiles=n_lat_tiles,
                             n_rows=n_lat_tiles * NA_ROWS_PER_TILE)
    return pl.pallas_call(
        kern,
        grid=(b, n_groups, n_q_tiles),
        in_specs=[
            pl.BlockSpec((None, TOK_TILE, width), lambda bi, hg, i: (bi, i, hg)),
            pl.BlockSpec((None, n_tok, width), lambda bi, hg, i: (bi, 0, n_groups + hg)),
            pl.BlockSpec((None, n_tok, width), lambda bi, hg, i: (bi, 0, 2 * n_groups + hg)),
            pl.BlockSpec((2 * NA_PAIRS_PER_STEP, N_DR - 1, GRID_W, 2 * GRID_W),
                         lambda bi, hg, i: (hg, 0, 0, 0)),
        ],
        out_specs=pl.BlockSpec((None, TOK_TILE, width), lambda bi, hg, i: (bi, i, hg)),
        out_shape=jax.ShapeDtypeStruct((b, n_q_tiles * TOK_TILE, NA_WIDTH), BF16),
        compiler_params=_cparams(3),
        name="na_attn",
    )(na_qkv, na_qkv, na_qkv, bias_slabs)


def _gqa_finish(o_ref, accs):
    outs = [a[0:HEAD_DIM] / a[HEAD_DIM:HEAD_DIM + 1] for a in accs]
    o_ref[...] = jnp.concatenate(outs, axis=0).T.astype(BF16)


def _gqa_bounded_kernel(q_ref, k_ref, v_ref, sh_ref, o_ref, qx_sc, p_sc, l_sc, acc_sc, *,
                        n_lat_tiles, group):
    i = pl.program_id(2)
    n_key_tiles = n_lat_tiles + 1
    row = lax.broadcasted_iota(jnp.int32, (K_EXT - HEAD_DIM, TOK_TILE), 0)
    for hd in range(GQA_REP):
        qx_sc[hd, 0:HEAD_DIM] = q_ref[hd]
        qx_sc[hd, HEAD_DIM:K_EXT] = jnp.where(row == 0, -sh_ref[hd][:, 0:1], 0.0).astype(BF16)
    l_sc[...] = jnp.zeros(l_sc.shape, F32)

    def key_tile(c):
        return k_ref[pl.ds(pl.multiple_of(c * TOK_TILE, TOK_TILE), TOK_TILE), :]

    def value_tile(c):
        return v_ref[c, 0:HEAD_DIM, :]

    def probs(k_c, hd):
        p = jnp.exp2(_dot(k_c, qx_sc[hd]))
        l_sc[hd] = l_sc[hd] + jnp.sum(p.reshape(TOK_TILE // F32_SUBLANES, F32_SUBLANES, TOK_TILE),
                                      axis=0)
        return p.astype(BF16)

    def finish(accs):
        outs = [accs[hd] / jnp.sum(l_sc[hd], axis=0, keepdims=True) for hd in range(GQA_REP)]
        o_ref[...] = jnp.concatenate(outs, axis=0).T.astype(BF16)

    @pl.when(i < n_lat_tiles)
    def _latent():
        acc_sc[...] = jnp.zeros(acc_sc.shape, F32)
        p_sc[...] = jnp.zeros(p_sc.shape, BF16)

        def tiles(gi, carry):
            c0 = gi * group
            p_prev = [p_sc[hd] for hd in range(GQA_REP)]
            c_prev = jnp.maximum(c0 - 1, 0)
            for cc in range(group):
                k_c = key_tile(c0 + cc)
                v_p = value_tile(c_prev)
                p_new = []
                for hd in range(GQA_REP):
                    p_new.append(probs(k_c, hd))
                    acc_sc[hd] = acc_sc[hd] + _dot(v_p, p_prev[hd])
                p_prev = p_new
                c_prev = c0 + cc
            for hd in range(GQA_REP):
                p_sc[hd] = p_prev[hd]
            return carry

        lax.fori_loop(0, n_key_tiles // group, tiles, 0)
        v_p = value_tile(n_key_tiles - 1)
        finish([acc_sc[hd] + _dot(v_p, p_sc[hd]) for hd in range(GQA_REP)])

    @pl.when(i >= n_lat_tiles)
    def _context():
        k_c = key_tile(n_lat_tiles)
        v_c = value_tile(n_lat_tiles)
        finish([_dot(v_c, probs(k_c, hd)) for hd in range(GQA_REP)])


def _gqa_kernel(q_ref, k_ref, v_ref, o_ref, s_a, s_b, cm_a, cm_b, p_a, p_b, m_sc, acc_sc, *,
                n_lat_tiles):
    i = pl.program_id(2)

    def key_tile(c):
        return k_ref[pl.ds(pl.multiple_of(c * TOK_TILE, TOK_TILE), TOK_TILE), 0:HEAD_DIM]

    def finish(accs):
        _gqa_finish(o_ref, accs)

    def step(c_prev, c_next, s_cur, cm_cur, s_nxt, cm_nxt, p_prev, p_cur):
        v_prev = v_ref[c_prev]
        k_next = key_tile(c_next)
        for hd in range(GQA_REP):
            pv = _dot(v_prev, p_prev[hd])
            s = _dot(k_next, q_ref[hd])
            s_nxt[hd] = s
            cm_nxt[hd] = jnp.max(s, axis=0, keepdims=True)
            m_old = m_sc[hd]
            m_new = jnp.maximum(m_old, cm_cur[hd])
            alpha = jnp.exp2(m_old - m_new)
            p_cur[hd] = jnp.exp2(s_cur[hd] - m_new).astype(BF16)
            acc_sc[hd] = (acc_sc[hd] + pv) * alpha
            m_sc[hd] = m_new

    @pl.when(i < n_lat_tiles)
    def _latent():
        m_sc[...] = jnp.full(m_sc.shape, -jnp.inf, F32)
        acc_sc[...] = jnp.zeros(acc_sc.shape, F32)
        p_b[...] = jnp.zeros(p_b.shape, BF16)
        k_0 = key_tile(0)
        for hd in range(GQA_REP):
            s = _dot(k_0, q_ref[hd])
            s_a[hd] = s
            cm_a[hd] = jnp.max(s, axis=0, keepdims=True)

        def pair(j, carry):
            c = 2 * j
            step(jnp.maximum(c - 1, 0), c + 1, s_a, cm_a, s_b, cm_b, p_b, p_a)
            step(c, c + 2, s_b, cm_b, s_a, cm_a, p_a, p_b)
            return carry

        lax.fori_loop(0, n_lat_tiles // 2, pair, 0, unroll=8)
        v_last = v_ref[n_lat_tiles - 1]
        v_ctx = v_ref[n_lat_tiles]
        accs = []
        for hd in range(GQA_REP):
            pv = _dot(v_last, p_b[hd])
            m_old = m_sc[hd]
            m_new = jnp.maximum(m_old, cm_a[hd])
            alpha = jnp.exp2(m_old - m_new)
            p = jnp.exp2(s_a[hd] - m_new).astype(BF16)
            accs.append((acc_sc[hd] + pv) * alpha + _dot(v_ctx, p))
        finish(accs)

    @pl.when(i >= n_lat_tiles)
    def _context():
        k_c = key_tile(n_lat_tiles)
        v_c = v_ref[n_lat_tiles]
        accs = []
        for hd in range(GQA_REP):
            s = _dot(k_c, q_ref[hd])
            p = jnp.exp2(s - jnp.max(s, axis=0, keepdims=True)).astype(BF16)
            accs.append(_dot(v_c, p))
        finish(accs)


def _gqa_specs(gq, gv, n_q_tiles):
    b = gq.shape[0]
    n_tok = gq.shape[-1]
    n_key_tiles = gv.shape[2]
    in_specs = [
        pl.BlockSpec((None, GQA_REP, HEAD_DIM, TOK_TILE), lambda bi, g, i: (bi, g, 0, i)),
        pl.BlockSpec((None, None, n_tok, K_EXT), lambda bi, g, i: (bi, g, 0, 0)),
        pl.BlockSpec((None, None, n_key_tiles, V_ROWS, TOK_TILE), lambda bi, g, i: (bi, g, 0, 0, 0)),
    ]
    out_spec = pl.BlockSpec((None, TOK_TILE, GQA_REP * HEAD_DIM), lambda bi, g, i: (bi, i, g))
    out_shape = jax.ShapeDtypeStruct((b, n_q_tiles * TOK_TILE, GQA_Q_WIDTH), BF16)
    return (b, GQA_KV_HEADS, n_q_tiles), in_specs, out_spec, out_shape


def _gqa_online_call(gq, gk, gv, n_q_tiles, n_lat_tiles):
    assert n_lat_tiles % 2 == 0 and gv.shape[2] == n_lat_tiles + 1
    grid, in_specs, out_spec, out_shape = _gqa_specs(gq, gv, n_q_tiles)
    score_buf = pltpu.VMEM((GQA_REP, TOK_TILE, TOK_TILE), F32)
    stat_buf = pltpu.VMEM((GQA_REP, 1, TOK_TILE), F32)
    prob_buf = pltpu.VMEM((GQA_REP, TOK_TILE, TOK_TILE), BF16)
    return pl.pallas_call(
        functools.partial(_gqa_kernel, n_lat_tiles=n_lat_tiles),
        grid=grid, in_specs=in_specs, out_specs=out_spec, out_shape=out_shape,
        scratch_shapes=[score_buf, score_buf, stat_buf, stat_buf, prob_buf, prob_buf, stat_buf,
                        pltpu.VMEM((GQA_REP, V_ROWS, TOK_TILE), F32)],
        compiler_params=_cparams(3),
        name="gqa_attn_online",
    )(gq, gk, gv)


def _gqa_bounded_call(gq, gk, gv, shift, n_q_tiles, n_lat_tiles):
    n_key_tiles = gv.shape[2]
    assert n_key_tiles == n_lat_tiles + 1
    group = max(g for g in range(1, GQA_MAX_TILE_GROUP + 1) if n_key_tiles % g == 0)
    grid, in_specs, out_spec, out_shape = _gqa_specs(gq, gv, n_q_tiles)
    in_specs.append(pl.BlockSpec((None, GQA_REP, 1, 128), lambda bi, g, i: (bi, g, 0, 0)))
    return pl.pallas_call(
        functools.partial(_gqa_bounded_kernel, n_lat_tiles=n_lat_tiles, group=group),
        grid=grid, in_specs=in_specs, out_specs=out_spec, out_shape=out_shape,
        scratch_shapes=[pltpu.VMEM((GQA_REP, K_EXT, TOK_TILE), BF16),
                        pltpu.VMEM((GQA_REP, TOK_TILE, TOK_TILE), BF16),
                        pltpu.VMEM((GQA_REP, F32_SUBLANES, TOK_TILE), F32),
                        pltpu.VMEM((GQA_REP, HEAD_DIM, TOK_TILE), F32)],
        compiler_params=_cparams(3),
        name="gqa_attn_bounded",
    )(gq, gk, gv, shift)


def _gqa_call(gq, gk, gv, stats, n_q_tiles, n_lat_tiles):
    n2 = jnp.max(stats[:, :, :, 0], axis=1)
    q_max = jnp.sqrt(n2[:, 0:GQA_Q_HEADS])
    k_max = jnp.repeat(jnp.sqrt(n2[:, GQA_Q_HEADS:GQA_Q_HEADS + GQA_KV_HEADS]), GQA_REP, axis=1)
    bound = q_max * k_max * BOUND_SLACK
    shift = jnp.broadcast_to(bound[:, :, None, None], bound.shape + (1, 128))
    return lax.cond(
        jnp.max(bound) <= MAX_SHIFT,
        lambda: _gqa_bounded_call(gq, gk, gv, shift, n_q_tiles, n_lat_tiles),
        lambda: _gqa_online_call(gq, gk, gv, n_q_tiles, n_lat_tiles))


def _mix_ffn_kernel(tok_ref, mod_ref, n1_ref, n2_ref, ya_ref, yb_ref, wg_ref, wpa_ref, wpb_ref,
                    wo_ref, win_ref, wout_ref, fn_ref, *rest, final_norm):
    n_par = len(rest) - 2 if final_norm else (len(rest) - 2 - 5)
    next_params = rest[0:n_par]
    o_ref = rest[n_par]
    next_outs = rest[n_par + 1:-1]
    g_sc = rest[-1]
    x = tok_ref[...]
    h = _rms_mod(x, n1_ref[...], mod_ref[0:1, :], mod_ref[1:2, :]).astype(BF16)
    gates = _dot(h, wg_ref[...])
    pa = _dot(ya_ref[...], wpa_ref[...])
    pb = _dot(yb_ref[...], wpb_ref[...])
    merged = (jax.nn.sigmoid(gates[:, 0:D_MODEL]) * pa
              + jax.nn.sigmoid(gates[:, D_MODEL:2 * D_MODEL]) * pb)
    x = x + mod_ref[2:3, :] * _dot(merged.astype(BF16), wo_ref[...])
    h = _rms_mod(x, n2_ref[...], mod_ref[3:4, :], mod_ref[4:5, :]).astype(BF16)
    for j in range(D_FF // FF_CHUNK):
        c0 = j * FF_CHUNK
        a = _dot(h, win_ref[:, c0:c0 + FF_CHUNK])
        u = _dot(h, win_ref[:, D_FF + c0:D_FF + c0 + FF_CHUNK])
        g_sc[:, c0:c0 + FF_CHUNK] = (a * jax.nn.sigmoid(a) * u).astype(BF16)
    y = x + mod_ref[5:6, :] * _dot(g_sc[...], wout_ref[...])
    if final_norm:
        ms = jnp.mean(y * y, axis=-1, keepdims=True)
        y = y * lax.rsqrt(ms + EPS) * fn_ref[...]
    o_ref[...] = y
    if not final_norm:
        _inproj_body(y, *next_params, *next_outs)


def _mix_ffn_call(tok, mod, n1, n2, ya, yb, wg, wpa, wpb, wo, win, wout, fn, next_params, n_tiles,
                  n_lat_tiles):
    b = tok.shape[0]
    final_norm = next_params is None
    const = _const_spec
    tile = lambda width: pl.BlockSpec((None, TOK_TILE, width), lambda bi, i: (bi, i, 0))
    in_specs = [
        tile(D_MODEL), _mod_spec(n_lat_tiles), const((1, D_MODEL)), const((1, D_MODEL)),
        tile(NA_WIDTH), tile(GQA_Q_WIDTH),
        const((D_MODEL, 2 * D_MODEL)), const((NA_WIDTH, D_MODEL)), const((GQA_Q_WIDTH, D_MODEL)),
        const((D_MODEL, D_MODEL)), const((D_MODEL, 2 * D_FF)), const((D_FF, D_MODEL)),
        const((1, D_MODEL)),
    ]
    out_specs = [tile(D_MODEL)]
    out_shape = [jax.ShapeDtypeStruct((b, n_tiles * TOK_TILE, D_MODEL), F32)]
    operands = [tok, mod, n1, n2, ya, yb, wg, wpa, wpb, wo, win, wout, fn]
    if not final_norm:
        in_specs += _inproj_param_specs(n_lat_tiles)
        operands += list(next_params)
        specs, shapes = _inproj_out_specs(b, n_tiles)
        out_specs += specs
        out_shape += shapes
    return pl.pallas_call(
        functools.partial(_mix_ffn_kernel, final_norm=final_norm),
        grid=(b, n_tiles),
        in_specs=in_specs,
        out_specs=out_specs,
        out_shape=out_shape,
        scratch_shapes=[pltpu.VMEM((TOK_TILE, D_FF), BF16)],
        compiler_params=_cparams(2),
        name="mix_ffn",
    )(*operands)


def _rope_tables(n_lat, n_ctx):
    t = jnp.arange(n_lat)
    row = (t // GRID_W).astype(F32)
    col = (t % GRID_W).astype(F32)
    half = HEAD_DIM // 2
    inv = ROPE_THETA ** (-jnp.arange(0, half, 2, dtype=F32) / half)
    ang = jnp.concatenate([row[:, None] * inv, col[:, None] * inv], axis=-1)
    cos_t = jnp.concatenate([jnp.cos(ang).T, jnp.ones((half, n_ctx), F32)], axis=1)
    sin_t = jnp.concatenate([jnp.sin(ang).T, jnp.zeros((half, n_ctx), F32)], axis=1)
    return cos_t, sin_t


def kernel(x, c, ctx, c_ctx, w_mod, b_mod, norm1, w_in, na_rpb, q_gain, k_gain, w_pa, w_pb, w_o,
           norm2, w_ffn_in, w_ffn_out, final_norm):
    b, s, d = x.shape
    n_ctx = ctx.shape[1]
    assert d == D_MODEL and n_ctx == CTX_LEN == TOK_TILE and s % TOK_TILE == 0
    assert s // GRID_W >= NA_WIN_H
    n_lat_tiles = s // TOK_TILE
    n_tiles = n_lat_tiles + 1

    perm = np.concatenate([np.arange(0, HEAD_DIM, 2), np.arange(1, HEAD_DIM, 2)])
    q_cols = (np.arange(GQA_Q_HEADS)[:, None] * HEAD_DIM + perm[None, :]).reshape(-1)
    k_cols = (np.arange(GQA_KV_HEADS)[:, None] * HEAD_DIM + perm[None, :]).reshape(-1)
    o_gq = 3 * NA_WIDTH
    o_gk = o_gq + GQA_Q_WIDTH
    o_gv = o_gk + GQA_KV_WIDTH
    o_ga = o_gv + GQA_KV_WIDTH

    cos_t, sin_t = _rope_tables(s, n_ctx)
    tok = jnp.concatenate([x, ctx], axis=1)

    n_rows_cc = 8
    cc = jnp.zeros((n_rows_cc, D_MODEL), F32).at[0:b].set(c).at[b].set(c_ctx)
    mod_all = _mod_call(cc, w_mod, b_mod)
    mod_lat = mod_all[:, 0:b].reshape(DEPTH, b, 1, 6, D_MODEL)
    mod_ctx = jnp.broadcast_to(mod_all[:, b].reshape(DEPTH, 1, 1, 6, D_MODEL),
                               (DEPTH, b, 1, 6, D_MODEL))
    mod_sel = jnp.concatenate([mod_lat, mod_ctx], axis=2)

    bias_slabs = _bias_call(na_rpb).reshape(DEPTH, NA_HEADS, N_DR - 1, GRID_W, 2 * GRID_W)

    def inproj_params(l):
        w_l = w_in[l]
        wna = w_l[:, 0:o_gq].astype(BF16)
        wgt = jnp.concatenate(
            [w_l[:, o_gq:o_gk][:, q_cols], w_l[:, o_gk:o_gv][:, k_cols], w_l[:, o_gv:o_ga]],
            axis=1).T.astype(BF16)
        return (mod_sel[l], norm1[l].reshape(1, D_MODEL), wna, wgt,
                q_gain[l][perm].reshape(HEAD_DIM, 1), k_gain[l][perm].reshape(HEAD_DIM, 1),
                cos_t, sin_t)

    mixer_in = _inproj_call(tok, inproj_params(0), n_lat_tiles)
    for l in range(DEPTH):
        last = l == DEPTH - 1
        na_qkv, gq, gk, gv, stats = mixer_in
        n_q_tiles = n_lat_tiles if last else n_tiles
        ya = _na_call(na_qkv, bias_slabs[l], n_q_tiles, n_lat_tiles)
        yb = _gqa_call(gq, gk, gv, stats, n_q_tiles, n_lat_tiles)
        outs = _mix_ffn_call(tok, mod_sel[l], norm1[l].reshape(1, D_MODEL),
                             norm2[l].reshape(1, D_MODEL), ya, yb, w_in[l][:, o_ga:].astype(BF16),
                             w_pa[l].astype(BF16), w_pb[l].astype(BF16), w_o[l].astype(BF16),
                             w_ffn_in[l].astype(BF16), w_ffn_out[l].astype(BF16),
                             final_norm.reshape(1, D_MODEL),
                             None if last else inproj_params(l + 1), n_q_tiles, n_lat_tiles)
        tok = outs[0]
        mixer_in = outs[1:]
    return tok
```

```python
import functools
import math

import jax
import jax.numpy as jnp
import numpy as np
from jax import lax
from jax.experimental import pallas as pl
from jax.experimental.pallas import tpu as pltpu

D_MODEL = 1024
DEPTH = 4
GRID_W = 64
CTX_LEN = 256
HEAD_DIM = 64
NA_HEADS = 8
NA_WIN_H = 8
NA_WIN_W = 16
GQA_Q_HEADS = 8
GQA_KV_HEADS = 2
GQA_REP = GQA_Q_HEADS // GQA_KV_HEADS
NA_WIDTH = NA_HEADS * HEAD_DIM
GQA_Q_WIDTH = GQA_Q_HEADS * HEAD_DIM
GQA_KV_WIDTH = GQA_KV_HEADS * HEAD_DIM
D_FF = -(-(8 * D_MODEL) // (3 * 256)) * 256
ROPE_THETA = 10000.0
EPS = 1e-6
SCALE = HEAD_DIM ** -0.5
LOG2E = math.log2(math.e)
QK_SCALE = SCALE * LOG2E

TOK_TILE = 256
NA_ROWS_PER_TILE = TOK_TILE // GRID_W
NA_KEYS_WIN = NA_WIN_H * GRID_W
N_DR = 2 * NA_WIN_H - 1
N_DC = 2 * NA_WIN_W - 1
FF_CHUNK = 256
NA_PAIRS_PER_STEP = 4
K_EXT = 2 * HEAD_DIM
GQA_MAX_TILE_GROUP = 11
BOUND_SLACK = 1.01
MAX_SHIFT = 60.0
F32_SUBLANES = 8
BF16_SUBLANES = 16
V_ROWS = HEAD_DIM + BF16_SUBLANES
VMEM_LIMIT_BYTES = 56 * 1024 * 1024

BF16 = jnp.bfloat16
F32 = jnp.float32


def _cparams(n_grid_dims):
    return pltpu.CompilerParams(
        dimension_semantics=("arbitrary",) * n_grid_dims,
        vmem_limit_bytes=VMEM_LIMIT_BYTES)


def _nt_dot(a, b):
    return lax.dot_general(a, b, (((1,), (1,)), ((), ())), preferred_element_type=F32)


def _dot(a, b):
    return jnp.dot(a, b, preferred_element_type=F32)


def _rms_mod(x, g, shift, scale):
    ms = jnp.mean(x * x, axis=-1, keepdims=True)
    y = x * lax.rsqrt(ms + EPS) * g
    return y * (1.0 + scale) + shift


def _mod_kernel(cc_ref, w_ref, b_ref, o_ref):
    cc = cc_ref[...]
    a = (cc * jax.nn.sigmoid(cc)).astype(BF16)
    o_ref[...] = _dot(a, w_ref[...].astype(BF16)) + b_ref[...]


def _mod_call(cc, w_mod, b_mod):
    n_rows = cc.shape[0]
    return pl.pallas_call(
        _mod_kernel,
        grid=(DEPTH, 6),
        in_specs=[
            pl.BlockSpec((n_rows, D_MODEL), lambda l, j: (0, 0)),
            pl.BlockSpec((None, D_MODEL, D_MODEL), lambda l, j: (l, 0, j)),
            pl.BlockSpec((None, 1, D_MODEL), lambda l, j: (l, 0, j)),
        ],
        out_specs=pl.BlockSpec((None, n_rows, D_MODEL), lambda l, j: (l, 0, j)),
        out_shape=jax.ShapeDtypeStruct((DEPTH, n_rows, 6 * D_MODEL), F32),
        compiler_params=_cparams(2),
        name="adaln_mod",
    )(cc, w_mod, b_mod.reshape(DEPTH, 1, 6 * D_MODEL))


def _bias_kernel(rpb_ref, o_ref):
    lh = pl.program_id(0)
    base = lh * (N_DR * N_DC)
    wq = lax.broadcasted_iota(jnp.int32, (GRID_W, 2 * GRID_W), 0)
    lane = lax.broadcasted_iota(jnp.int32, (GRID_W, 2 * GRID_W), 1)
    upper = lane >= GRID_W
    wk = jnp.bitwise_and(lane, GRID_W - 1)
    dci = jnp.clip(wk - wq, -(NA_WIN_W - 1), NA_WIN_W - 1) + (NA_WIN_W - 1)
    c_start = jnp.clip(wq - NA_WIN_W // 2, 0, GRID_W - NA_WIN_W)
    in_win = jnp.logical_and(wk >= c_start, wk < c_start + NA_WIN_W)

    def body(d, carry):
        acc = jnp.zeros((GRID_W, 2 * GRID_W), F32)
        for dc in range(N_DC):
            v_lo = rpb_ref[base + d * N_DC + dc]
            v_hi = rpb_ref[base + (d + 1) * N_DC + dc]
            acc = jnp.where(dci == dc, jnp.where(upper, v_hi, v_lo), acc)
        o_ref[d] = jnp.where(in_win, acc * LOG2E, -jnp.inf)
        return carry

    lax.fori_loop(0, N_DR - 1, body, 0)


def _bias_call(na_rpb):
    rpb_flat = na_rpb.astype(F32).reshape(-1)
    return pl.pallas_call(
        _bias_kernel,
        grid=(DEPTH * NA_HEADS,),
        in_specs=[pl.BlockSpec(memory_space=pltpu.SMEM)],
        out_specs=pl.BlockSpec((None, N_DR - 1, GRID_W, 2 * GRID_W), lambda i: (i, 0, 0, 0)),
        out_shape=jax.ShapeDtypeStruct((DEPTH * NA_HEADS, N_DR - 1, GRID_W, 2 * GRID_W), F32),
        compiler_params=_cparams(1),
        name="na_bias_slabs",
    )(rpb_flat)


def _token_tile(lat_ref, ctx_ref, n_lat_tiles):
    return jnp.where(pl.program_id(1) < n_lat_tiles, lat_ref[...], ctx_ref[...])


def _token_specs(lat, ctx_src, n_lat_tiles):
    ctx_block = 0 if ctx_src.shape[1] == CTX_LEN else n_lat_tiles
    return [
        pl.BlockSpec((None, TOK_TILE, D_MODEL),
                     lambda bi, i: (bi, jnp.minimum(i, n_lat_tiles - 1), 0)),
        pl.BlockSpec((None, TOK_TILE, D_MODEL), lambda bi, i: (bi, ctx_block, 0)),
    ]


def _inproj_kernel(lat_ref, ctx_ref, mod_ref, n1_ref, wna_ref, wgt_ref, qg_ref, kg_ref, cos_ref,
                   sin_ref, na_ref, gq_ref, gk_ref, gv_ref, st_ref, *, n_lat_tiles):
    _inproj_body(_token_tile(lat_ref, ctx_ref, n_lat_tiles), mod_ref, n1_ref, wna_ref, wgt_ref,
                 qg_ref, kg_ref, cos_ref, sin_ref, na_ref, gq_ref, gk_ref, gv_ref, st_ref)


def _inproj_body(x, mod_ref, n1_ref, wna_ref, wgt_ref, qg_ref, kg_ref, cos_ref, sin_ref,
                 na_ref, gq_ref, gk_ref, gv_ref, st_ref):
    h = _rms_mod(x, n1_ref[...], mod_ref[0:1, :], mod_ref[1:2, :]).astype(BF16)

    na = _dot(h, wna_ref[...])
    na_ref[:, 0:NA_WIDTH] = (na[:, 0:NA_WIDTH] * QK_SCALE).astype(BF16)
    na_ref[:, NA_WIDTH:3 * NA_WIDTH] = na[:, NA_WIDTH:3 * NA_WIDTH].astype(BF16)

    gt = _nt_dot(wgt_ref[...], h)
    cos = cos_ref[...]
    sin = sin_ref[...]
    half = HEAD_DIM // 2

    def norm_rope(xh, gain):
        ms = jnp.sum(xh * xh, axis=0, keepdims=True) * (1.0 / HEAD_DIM)
        y = xh * lax.rsqrt(ms + EPS) * gain
        y1 = y[0:half]
        y2 = y[half:HEAD_DIM]
        return jnp.concatenate([y1 * cos - y2 * sin, y1 * sin + y2 * cos], axis=0)

    def max_sq_norm(y):
        n2 = jnp.sum(y * y, axis=0, keepdims=True)
        return jnp.broadcast_to(jnp.max(n2, axis=1, keepdims=True), (1, 128))

    st_ref[...] = jnp.zeros(st_ref.shape, F32)
    qg = qg_ref[...]
    for hd in range(GQA_Q_HEADS):
        q = norm_rope(gt[hd * HEAD_DIM:(hd + 1) * HEAD_DIM], qg) * QK_SCALE
        gq_ref[hd] = q.astype(BF16)
        st_ref[hd:hd + 1, :] = max_sq_norm(q)
    kg = kg_ref[...]
    ones_col = (lax.broadcasted_iota(jnp.int32, (K_EXT - HEAD_DIM, TOK_TILE), 0) == 0).astype(F32)
    for kv in range(GQA_KV_HEADS):
        r0 = GQA_Q_WIDTH + kv * HEAD_DIM
        k = norm_rope(gt[r0:r0 + HEAD_DIM], kg)
        st_ref[GQA_Q_HEADS + kv:GQA_Q_HEADS + kv + 1, :] = max_sq_norm(k)
        gk_ref[kv] = jnp.concatenate([k, ones_col], axis=0).T.astype(BF16)
        r1 = GQA_Q_WIDTH + GQA_KV_WIDTH + kv * HEAD_DIM
        gv_ref[kv, 0:HEAD_DIM] = gt[r1:r1 + HEAD_DIM].astype(BF16)
        gv_ref[kv, HEAD_DIM:V_ROWS] = jnp.ones((V_ROWS - HEAD_DIM, TOK_TILE), BF16)


def _const_spec(shp):
    return pl.BlockSpec(shp, lambda bi, i: (0,) * len(shp), pipeline_mode=pl.Buffered(1))


def _mod_spec(n_lat_tiles):
    return pl.BlockSpec((None, None, 6, D_MODEL), lambda bi, i: (bi, i // n_lat_tiles, 0, 0))


def _inproj_param_specs(n_lat_tiles):
    return [
        _mod_spec(n_lat_tiles),
        _const_spec((1, D_MODEL)),
        _const_spec((D_MODEL, 3 * NA_WIDTH)),
        _const_spec((GQA_Q_WIDTH + 2 * GQA_KV_WIDTH, D_MODEL)),
        _const_spec((HEAD_DIM, 1)),
        _const_spec((HEAD_DIM, 1)),
        pl.BlockSpec((HEAD_DIM // 2, TOK_TILE), lambda bi, i: (0, i)),
        pl.BlockSpec((HEAD_DIM // 2, TOK_TILE), lambda bi, i: (0, i)),
    ]


def _inproj_out_specs(b, nt):
    n_tok = nt * TOK_TILE
    specs = [
        pl.BlockSpec((None, TOK_TILE, 3 * NA_WIDTH), lambda bi, i: (bi, i, 0)),
        pl.BlockSpec((None, GQA_Q_HEADS, HEAD_DIM, TOK_TILE), lambda bi, i: (bi, 0, 0, i)),
        pl.BlockSpec((None, GQA_KV_HEADS, TOK_TILE, K_EXT), lambda bi, i: (bi, 0, i, 0)),
        pl.BlockSpec((None, GQA_KV_HEADS, None, V_ROWS, TOK_TILE), lambda bi, i: (bi, 0, i, 0, 0)),
        pl.BlockSpec((None, None, 16, 128), lambda bi, i: (bi, i, 0, 0)),
    ]
    shapes = [
        jax.ShapeDtypeStruct((b, n_tok, 3 * NA_WIDTH), BF16),
        jax.ShapeDtypeStruct((b, GQA_Q_HEADS, HEAD_DIM, n_tok), BF16),
        jax.ShapeDtypeStruct((b, GQA_KV_HEADS, n_tok, K_EXT), BF16),
        jax.ShapeDtypeStruct((b, GQA_KV_HEADS, nt, V_ROWS, TOK_TILE), BF16),
        jax.ShapeDtypeStruct((b, nt, 16, 128), F32),
    ]
    return specs, shapes


def _inproj_call(lat, ctx_src, params, n_lat_tiles):
    b = lat.shape[0]
    nt = n_lat_tiles + 1
    out_specs, out_shape = _inproj_out_specs(b, nt)
    return pl.pallas_call(
        functools.partial(_inproj_kernel, n_lat_tiles=n_lat_tiles),
        grid=(b, nt),
        in_specs=_token_specs(lat, ctx_src, n_lat_tiles) + _inproj_param_specs(n_lat_tiles),
        out_specs=out_specs,
        out_shape=out_shape,
        compiler_params=_cparams(2),
        name="in_proj",
    )(lat, ctx_src, *params)


def _na_kernel(q_ref, k_ref, v_ref, w_ref, o_ref, *, n_lat_tiles, n_rows):
    i = pl.program_id(2)
    n_lat = n_lat_tiles * TOK_TILE
    pair_w = 2 * HEAD_DIM
    lane = lax.broadcasted_iota(jnp.int32, (1, pair_w), 1)
    lo_half = lane < HEAD_DIM
    pairs = range(NA_PAIRS_PER_STEP)
    cols = [slice(pp * pair_w, (pp + 1) * pair_w) for pp in pairs]

    def with_ones(v):
        return jnp.concatenate([v, jnp.ones(v.shape, BF16)], axis=1)

    def both_heads(x, rr):
        lo = rr * GRID_W
        return jnp.concatenate([x[lo:lo + GRID_W], x[TOK_TILE + lo:TOK_TILE + lo + GRID_W]], axis=0)

    def normalise(o):
        return o[:, 0:pair_w] / o[:, pair_w:2 * pair_w]

    q2s = []
    s_ctxs = []
    v_ctxs = []
    for pp in pairs:
        q = q_ref[:, cols[pp]]
        zero = jnp.zeros_like(q)
        q2 = jnp.concatenate([jnp.where(lo_half, q, zero), jnp.where(lo_half, zero, q)], axis=0)
        q2s.append(q2)
        s_ctxs.append(_nt_dot(q2, k_ref[n_lat:n_lat + CTX_LEN, cols[pp]]))
        v_ctxs.append(with_ones(v_ref[n_lat:n_lat + CTX_LEN, cols[pp]]))

    @pl.when(i < n_lat_tiles)
    def _latent():
        units = [(pp, rr) for pp in pairs for rr in range(NA_ROWS_PER_TILE)]
        k0s = []
        d0s = []
        for rr in range(NA_ROWS_PER_TILE):
            r = i * NA_ROWS_PER_TILE + rr
            r_start = jnp.clip(r - NA_WIN_H // 2, 0, n_rows - NA_WIN_H)
            d0s.append(r_start - r + (NA_WIN_H - 1))
            k0s.append(pl.multiple_of(r_start * GRID_W, GRID_W))
        s_wins = {}
        for pp, rr in units:
            k_win = k_ref[pl.ds(k0s[rr], NA_KEYS_WIN), cols[pp]]
            bias = jnp.concatenate(
                [jnp.concatenate([w_ref[2 * pp + hh, d0s[rr] + 2 * jj]
                                  for jj in range(NA_WIN_H // 2)], axis=1)
                 for hh in range(2)], axis=0)
            s_wins[pp, rr] = _nt_dot(both_heads(q2s[pp], rr), k_win) + bias
        p_wins = {}
        p_ctxs = {}
        for pp, rr in units:
            s_w = s_wins[pp, rr]
            s_c = both_heads(s_ctxs[pp], rr)
            m = jnp.maximum(jnp.max(s_w, axis=-1, keepdims=True),
                            jnp.max(s_c, axis=-1, keepdims=True))
            p_wins[pp, rr] = jnp.exp2(s_w - m).astype(BF16)
            p_ctxs[pp, rr] = jnp.exp2(s_c - m).astype(BF16)
        for pp in pairs:
            rows = range(NA_ROWS_PER_TILE)
            p_ctx = jnp.concatenate([p_ctxs[pp, rr][0:GRID_W] for rr in rows]
                                    + [p_ctxs[pp, rr][GRID_W:] for rr in rows], axis=0)
            o_ctx = _dot(p_ctx, v_ctxs[pp])
            for rr in rows:
                v_win = with_ones(v_ref[pl.ds(k0s[rr], NA_KEYS_WIN), cols[pp]])
                o = normalise(_dot(p_wins[pp, rr], v_win) + both_heads(o_ctx, rr))
                o_ref[rr * GRID_W:(rr + 1) * GRID_W, cols[pp]] = jnp.where(
                    lo_half, o[0:GRID_W], o[GRID_W:]).astype(BF16)

    @pl.when(i >= n_lat_tiles)
    def _context():
        for pp in pairs:
            s = s_ctxs[pp]
            p = jnp.exp2(s - jnp.max(s, axis=-1, keepdims=True))
            o = normalise(_dot(p.astype(BF16), v_ctxs[pp]))
            o_ref[:, cols[pp]] = jnp.where(lo_half, o[0:TOK_TILE], o[TOK_TILE:]).astype(BF16)


def _na_call(na_qkv, bias_slabs, n_q_tiles, n_lat_tiles):
    b, n_tok, _ = na_qkv.shape
    n_groups = NA_HEADS // (2 * NA_PAIRS_PER_STEP)
    width = 2 * HEAD_DIM * NA_PAIRS_PER_STEP
    kern = functools.partial(_na_kernel, n_lat_tiles=n_lat_tiles,
                             n_rows=n_lat_tiles * NA_ROWS_PER_TILE)
    return pl.pallas_call(
        kern,
        grid=(b, n_groups, n_q_tiles),
        in_specs=[
            pl.BlockSpec((None, TOK_TILE, width), lambda bi, hg, i: (bi, i, hg)),
            pl.BlockSpec((None, n_tok, width), lambda bi, hg, i: (bi, 0, n_groups + hg),
                         pipeline_mode=pl.Buffered(1)),
            pl.BlockSpec((None, n_tok, width), lambda bi, hg, i: (bi, 0, 2 * n_groups + hg),
                         pipeline_mode=pl.Buffered(1)),
            pl.BlockSpec((2 * NA_PAIRS_PER_STEP, N_DR - 1, GRID_W, 2 * GRID_W),
                         lambda bi, hg, i: (hg, 0, 0, 0)),
        ],
        out_specs=pl.BlockSpec((None, TOK_TILE, width), lambda bi, hg, i: (bi, i, hg)),
        out_shape=jax.ShapeDtypeStruct((b, n_q_tiles * TOK_TILE, NA_WIDTH), BF16),
        compiler_params=_cparams(3),
        name="na_attn",
    )(na_qkv, na_qkv, na_qkv, bias_slabs)


def _gqa_finish(o_ref, accs):
    outs = [a[0:HEAD_DIM] / a[HEAD_DIM:HEAD_DIM + 1] for a in accs]
    o_ref[...] = jnp.concatenate(outs, axis=0).T.astype(BF16)


def _gqa_bounded_kernel(q_ref, k_ref, v_ref, sh_ref, o_ref, qx_sc, p_sc, l_sc, acc_sc, *,
                        n_lat_tiles, group):
    i = pl.program_id(2)
    n_key_tiles = n_lat_tiles + 1
    row = lax.broadcasted_iota(jnp.int32, (K_EXT - HEAD_DIM, TOK_TILE), 0)
    for hd in range(GQA_REP):
        qx_sc[hd, 0:HEAD_DIM] = q_ref[hd]
        qx_sc[hd, HEAD_DIM:K_EXT] = jnp.where(row == 0, -sh_ref[hd][:, 0:1], 0.0).astype(BF16)
    l_sc[...] = jnp.zeros(l_sc.shape, F32)

    def key_tile(c):
        return k_ref[pl.ds(pl.multiple_of(c * TOK_TILE, TOK_TILE), TOK_TILE), :]

    def value_tile(c):
        return v_ref[c, 0:HEAD_DIM, :]

    def probs(k_c, hd):
        p = jnp.exp2(_dot(k_c, qx_sc[hd]))
        l_sc[hd] = l_sc[hd] + jnp.sum(p.reshape(TOK_TILE // F32_SUBLANES, F32_SUBLANES, TOK_TILE),
                                      axis=0)
        return p.astype(BF16)

    def finish(accs):
        outs = [accs[hd] / jnp.sum(l_sc[hd], axis=0, keepdims=True) for hd in range(GQA_REP)]
        o_ref[...] = jnp.concatenate(outs, axis=0).T.astype(BF16)

    @pl.when(i < n_lat_tiles)
    def _latent():
        acc_sc[...] = jnp.zeros(acc_sc.shape, F32)
        p_sc[...] = jnp.zeros(p_sc.shape, BF16)

        def tiles(gi, carry):
            c0 = gi * group
            p_prev = [p_sc[hd] for hd in range(GQA_REP)]
            c_prev = jnp.maximum(c0 - 1, 0)
            for cc in range(group):
                k_c = key_tile(c0 + cc)
                v_p = value_tile(c_prev)
                p_new = []
                for hd in range(GQA_REP):
                    p_new.append(probs(k_c, hd))
                    acc_sc[hd] = acc_sc[hd] + _dot(v_p, p_prev[hd])
                p_prev = p_new
                c_prev = c0 + cc
            for hd in range(GQA_REP):
                p_sc[hd] = p_prev[hd]
            return carry

        lax.fori_loop(0, n_key_tiles // group, tiles, 0)
        v_p = value_tile(n_key_tiles - 1)
        finish([acc_sc[hd] + _dot(v_p, p_sc[hd]) for hd in range(GQA_REP)])

    @pl.when(i >= n_lat_tiles)
    def _context():
        k_c = key_tile(n_lat_tiles)
        v_c = value_tile(n_lat_tiles)
        finish([_dot(v_c, probs(k_c, hd)) for hd in range(GQA_REP)])


def _gqa_kernel(q_ref, k_ref, v_ref, o_ref, s_a, s_b, cm_a, cm_b, p_a, p_b, m_sc, acc_sc, *,
                n_lat_tiles):
    i = pl.program_id(2)

    def key_tile(c):
        return k_ref[pl.ds(pl.multiple_of(c * TOK_TILE, TOK_TILE), TOK_TILE), 0:HEAD_DIM]

    def finish(accs):
        _gqa_finish(o_ref, accs)

    def step(c_prev, c_next, s_cur, cm_cur, s_nxt, cm_nxt, p_prev, p_cur):
        v_prev = v_ref[c_prev]
        k_next = key_tile(c_next)
        for hd in range(GQA_REP):
            pv = _dot(v_prev, p_prev[hd])
            s = _dot(k_next, q_ref[hd])
            s_nxt[hd] = s
            cm_nxt[hd] = jnp.max(s, axis=0, keepdims=True)
            m_old = m_sc[hd]
            m_new = jnp.maximum(m_old, cm_cur[hd])
            alpha = jnp.exp2(m_old - m_new)
            p_cur[hd] = jnp.exp2(s_cur[hd] - m_new).astype(BF16)
            acc_sc[hd] = (acc_sc[hd] + pv) * alpha
            m_sc[hd] = m_new

    @pl.when(i < n_lat_tiles)
    def _latent():
        m_sc[...] = jnp.full(m_sc.shape, -jnp.inf, F32)
        acc_sc[...] = jnp.zeros(acc_sc.shape, F32)
        p_b[...] = jnp.zeros(p_b.shape, BF16)
        k_0 = key_tile(0)
        for hd in range(GQA_REP):
            s = _dot(k_0, q_ref[hd])
            s_a[hd] = s
            cm_a[hd] = jnp.max(s, axis=0, keepdims=True)

        def pair(j, carry):
            c = 2 * j
            step(jnp.maximum(c - 1, 0), c + 1, s_a, cm_a, s_b, cm_b, p_b, p_a)
            step(c, c + 2, s_b, cm_b, s_a, cm_a, p_a, p_b)
            return carry

        lax.fori_loop(0, n_lat_tiles // 2, pair, 0, unroll=8)
        v_last = v_ref[n_lat_tiles - 1]
        v_ctx = v_ref[n_lat_tiles]
        accs = []
        for hd in range(GQA_REP):
            pv = _dot(v_last, p_b[hd])
            m_old = m_sc[hd]
            m_new = jnp.maximum(m_old, cm_a[hd])
            alpha = jnp.exp2(m_old - m_new)
            p = jnp.exp2(s_a[hd] - m_new).astype(BF16)
            accs.append((acc_sc[hd] + pv) * alpha + _dot(v_ctx, p))
        finish(accs)

    @pl.when(i >= n_lat_tiles)
    def _context():
        k_c = key_tile(n_lat_tiles)
        v_c = v_ref[n_lat_tiles]
        accs = []
        for hd in range(GQA_REP):
            s = _dot(k_c, q_ref[hd])
            p = jnp.exp2(s - jnp.max(s, axis=0, keepdims=True)).astype(BF16)
            accs.append(_dot(v_c, p))
        finish(accs)


def _gqa_specs(gq, gv, n_q_tiles):
    b = gq.shape[0]
    n_tok = gq.shape[-1]
    n_key_tiles = gv.shape[2]
    in_specs = [
        pl.BlockSpec((None, GQA_REP, HEAD_DIM, TOK_TILE), lambda bi, g, i: (bi, g, 0, i)),
        pl.BlockSpec((None, None, n_tok, K_EXT), lambda bi, g, i: (bi, g, 0, 0)),
        pl.BlockSpec((None, None, n_key_tiles, V_ROWS, TOK_TILE), lambda bi, g, i: (bi, g, 0, 0, 0)),
    ]
    out_spec = pl.BlockSpec((None, TOK_TILE, GQA_REP * HEAD_DIM), lambda bi, g, i: (bi, i, g))
    out_shape = jax.ShapeDtypeStruct((b, n_q_tiles * TOK_TILE, GQA_Q_WIDTH), BF16)
    return (b, GQA_KV_HEADS, n_q_tiles), in_specs, out_spec, out_shape


def _gqa_online_call(gq, gk, gv, n_q_tiles, n_lat_tiles):
    assert n_lat_tiles % 2 == 0 and gv.shape[2] == n_lat_tiles + 1
    grid, in_specs, out_spec, out_shape = _gqa_specs(gq, gv, n_q_tiles)
    score_buf = pltpu.VMEM((GQA_REP, TOK_TILE, TOK_TILE), F32)
    stat_buf = pltpu.VMEM((GQA_REP, 1, TOK_TILE), F32)
    prob_buf = pltpu.VMEM((GQA_REP, TOK_TILE, TOK_TILE), BF16)
    return pl.pallas_call(
        functools.partial(_gqa_kernel, n_lat_tiles=n_lat_tiles),
        grid=grid, in_specs=in_specs, out_specs=out_spec, out_shape=out_shape,
        scratch_shapes=[score_buf, score_buf, stat_buf, stat_buf, prob_buf, prob_buf, stat_buf,
                        pltpu.VMEM((GQA_REP, V_ROWS, TOK_TILE), F32)],
        compiler_params=_cparams(3),
        name="gqa_attn_online",
    )(gq, gk, gv)


def _gqa_bounded_call(gq, gk, gv, shift, n_q_tiles, n_lat_tiles):
    n_key_tiles = gv.shape[2]
    assert n_key_tiles == n_lat_tiles + 1
    group = max(g for g in range(1, GQA_MAX_TILE_GROUP + 1) if n_key_tiles % g == 0)
    grid, in_specs, out_spec, out_shape = _gqa_specs(gq, gv, n_q_tiles)
    in_specs.append(pl.BlockSpec((None, GQA_REP, 1, 128), lambda bi, g, i: (bi, g, 0, 0)))
    return pl.pallas_call(
        functools.partial(_gqa_bounded_kernel, n_lat_tiles=n_lat_tiles, group=group),
        grid=grid, in_specs=in_specs, out_specs=out_spec, out_shape=out_shape,
        scratch_shapes=[pltpu.VMEM((GQA_REP, K_EXT, TOK_TILE), BF16),
                        pltpu.VMEM((GQA_REP, TOK_TILE, TOK_TILE), BF16),
                        pltpu.VMEM((GQA_REP, F32_SUBLANES, TOK_TILE), F32),
                        pltpu.VMEM((GQA_REP, HEAD_DIM, TOK_TILE), F32)],
        compiler_params=_cparams(3),
        name="gqa_attn_bounded",
    )(gq, gk, gv, shift)


def _gqa_call(gq, gk, gv, stats, n_q_tiles, n_lat_tiles):
    n2 = jnp.max(stats[:, :, :, 0], axis=1)
    q_max = jnp.sqrt(n2[:, 0:GQA_Q_HEADS])
    k_max = jnp.repeat(jnp.sqrt(n2[:, GQA_Q_HEADS:GQA_Q_HEADS + GQA_KV_HEADS]), GQA_REP, axis=1)
    bound = q_max * k_max * BOUND_SLACK
    shift = jnp.broadcast_to(bound[:, :, None, None], bound.shape + (1, 128))
    return lax.cond(
        jnp.max(bound) <= MAX_SHIFT,
        lambda: _gqa_bounded_call(gq, gk, gv, shift, n_q_tiles, n_lat_tiles),
        lambda: _gqa_online_call(gq, gk, gv, n_q_tiles, n_lat_tiles))


def _mix_ffn_kernel(lat_ref, ctx_ref, mod_ref, n1_ref, n2_ref, ya_ref, yb_ref, wg_ref, wpa_ref,
                    wpb_ref, wo_ref, win_ref, wout_ref, fn_ref, *rest, final_norm, n_lat_tiles):
    n_par = len(rest) - 2 if final_norm else (len(rest) - 2 - 5)
    next_params = rest[0:n_par]
    o_ref = rest[n_par]
    next_outs = rest[n_par + 1:-1]
    g_sc = rest[-1]
    x = _token_tile(lat_ref, ctx_ref, n_lat_tiles)
    h = _rms_mod(x, n1_ref[...], mod_ref[0:1, :], mod_ref[1:2, :]).astype(BF16)
    gates = _dot(h, wg_ref[...])
    pa = _dot(ya_ref[...], wpa_ref[...])
    pb = _dot(yb_ref[...], wpb_ref[...])
    merged = (jax.nn.sigmoid(gates[:, 0:D_MODEL]) * pa
              + jax.nn.sigmoid(gates[:, D_MODEL:2 * D_MODEL]) * pb)
    x = x + mod_ref[2:3, :] * _dot(merged.astype(BF16), wo_ref[...])
    h = _rms_mod(x, n2_ref[...], mod_ref[3:4, :], mod_ref[4:5, :]).astype(BF16)
    for j in range(D_FF // FF_CHUNK):
        c0 = j * FF_CHUNK
        a = _dot(h, win_ref[:, c0:c0 + FF_CHUNK])
        u = _dot(h, win_ref[:, D_FF + c0:D_FF + c0 + FF_CHUNK])
        g_sc[:, c0:c0 + FF_CHUNK] = (a * jax.nn.sigmoid(a) * u).astype(BF16)
    y = x + mod_ref[5:6, :] * _dot(g_sc[...], wout_ref[...])
    if final_norm:
        ms = jnp.mean(y * y, axis=-1, keepdims=True)
        y = y * lax.rsqrt(ms + EPS) * fn_ref[...]
    o_ref[...] = y
    if not final_norm:
        _inproj_body(y, *next_params, *next_outs)


def _mix_ffn_call(lat, ctx_src, mod, n1, n2, ya, yb, wg, wpa, wpb, wo, win, wout, fn, next_params,
                  n_tiles, n_lat_tiles):
    b = lat.shape[0]
    final_norm = next_params is None
    const = _const_spec
    tile = lambda width: pl.BlockSpec((None, TOK_TILE, width), lambda bi, i: (bi, i, 0))
    in_specs = _token_specs(lat, ctx_src, n_lat_tiles) + [
        _mod_spec(n_lat_tiles), const((1, D_MODEL)), const((1, D_MODEL)),
        tile(NA_WIDTH), tile(GQA_Q_WIDTH),
        const((D_MODEL, 2 * D_MODEL)), const((NA_WIDTH, D_MODEL)), const((GQA_Q_WIDTH, D_MODEL)),
        const((D_MODEL, D_MODEL)), const((D_MODEL, 2 * D_FF)), const((D_FF, D_MODEL)),
        const((1, D_MODEL)),
    ]
    out_specs = [tile(D_MODEL)]
    out_shape = [jax.ShapeDtypeStruct((b, n_tiles * TOK_TILE, D_MODEL), F32)]
    operands = [lat, ctx_src, mod, n1, n2, ya, yb, wg, wpa, wpb, wo, win, wout, fn]
    if not final_norm:
        in_specs += _inproj_param_specs(n_lat_tiles)
        operands += list(next_params)
        specs, shapes = _inproj_out_specs(b, n_tiles)
        out_specs += specs
        out_shape += shapes
    return pl.pallas_call(
        functools.partial(_mix_ffn_kernel, final_norm=final_norm, n_lat_tiles=n_lat_tiles),
        grid=(b, n_tiles),
        in_specs=in_specs,
        out_specs=out_specs,
        out_shape=out_shape,
        scratch_shapes=[pltpu.VMEM((TOK_TILE, D_FF), BF16)],
        compiler_params=_cparams(2),
        name="mix_ffn",
    )(*operands)


def _rope_tables(n_lat, n_ctx):
    t = jnp.arange(n_lat)
    row = (t // GRID_W).astype(F32)
    col = (t % GRID_W).astype(F32)
    half = HEAD_DIM // 2
    inv = ROPE_THETA ** (-jnp.arange(0, half, 2, dtype=F32) / half)
    ang = jnp.concatenate([row[:, None] * inv, col[:, None] * inv], axis=-1)
    cos_t = jnp.concatenate([jnp.cos(ang).T, jnp.ones((half, n_ctx), F32)], axis=1)
    sin_t = jnp.concatenate([jnp.sin(ang).T, jnp.zeros((half, n_ctx), F32)], axis=1)
    return cos_t, sin_t


def kernel(x, c, ctx, c_ctx, w_mod, b_mod, norm1, w_in, na_rpb, q_gain, k_gain, w_pa, w_pb, w_o,
           norm2, w_ffn_in, w_ffn_out, final_norm):
    b, s, d = x.shape
    n_ctx = ctx.shape[1]
    assert d == D_MODEL and n_ctx == CTX_LEN == TOK_TILE and s % TOK_TILE == 0
    assert s // GRID_W >= NA_WIN_H
    n_lat_tiles = s // TOK_TILE
    n_tiles = n_lat_tiles + 1

    perm = np.concatenate([np.arange(0, HEAD_DIM, 2), np.arange(1, HEAD_DIM, 2)])
    q_cols = (np.arange(GQA_Q_HEADS)[:, None] * HEAD_DIM + perm[None, :]).reshape(-1)
    k_cols = (np.arange(GQA_KV_HEADS)[:, None] * HEAD_DIM + perm[None, :]).reshape(-1)
    o_gq = 3 * NA_WIDTH
    o_gk = o_gq + GQA_Q_WIDTH
    o_gv = o_gk + GQA_KV_WIDTH
    o_ga = o_gv + GQA_KV_WIDTH

    cos_t, sin_t = _rope_tables(s, n_ctx)

    n_rows_cc = 8
    cc = jnp.zeros((n_rows_cc, D_MODEL), F32).at[0:b].set(c).at[b].set(c_ctx)
    mod_all = _mod_call(cc, w_mod, b_mod)
    mod_lat = mod_all[:, 0:b].reshape(DEPTH, b, 1, 6, D_MODEL)
    mod_ctx = jnp.broadcast_to(mod_all[:, b].reshape(DEPTH, 1, 1, 6, D_MODEL),
                               (DEPTH, b, 1, 6, D_MODEL))
    mod_sel = jnp.concatenate([mod_lat, mod_ctx], axis=2)

    bias_slabs = _bias_call(na_rpb).reshape(DEPTH, NA_HEADS, N_DR - 1, GRID_W, 2 * GRID_W)

    def inproj_params(l):
        w_l = w_in[l]
        wna = w_l[:, 0:o_gq].astype(BF16)
        wgt = jnp.concatenate(
            [w_l[:, o_gq:o_gk][:, q_cols], w_l[:, o_gk:o_gv][:, k_cols], w_l[:, o_gv:o_ga]],
            axis=1).T.astype(BF16)
        return (mod_sel[l], norm1[l].reshape(1, D_MODEL), wna, wgt,
                q_gain[l][perm].reshape(HEAD_DIM, 1), k_gain[l][perm].reshape(HEAD_DIM, 1),
                cos_t, sin_t)

    lat, ctx_src = x, ctx
    mixer_in = _inproj_call(lat, ctx_src, inproj_params(0), n_lat_tiles)
    for l in range(DEPTH):
        last = l == DEPTH - 1
        na_qkv, gq, gk, gv, stats = mixer_in
        n_q_tiles = n_lat_tiles if last else n_tiles
        ya = _na_call(na_qkv, bias_slabs[l], n_q_tiles, n_lat_tiles)
        yb = _gqa_call(gq, gk, gv, stats, n_q_tiles, n_lat_tiles)
        outs = _mix_ffn_call(lat, ctx_src, mod_sel[l], norm1[l].reshape(1, D_MODEL),
                             norm2[l].reshape(1, D_MODEL), ya, yb, w_in[l][:, o_ga:].astype(BF16),
                             w_pa[l].astype(BF16), w_pb[l].astype(BF16), w_o[l].astype(BF16),
                             w_ffn_in[l].astype(BF16), w_ffn_out[l].astype(BF16),
                             final_norm.reshape(1, D_MODEL),
                             None if last else inproj_params(l + 1), n_q_tiles, n_lat_tiles)
        lat = ctx_src = outs[0]
        mixer_in = outs[1:]
    return lat
```

```python
import functools
import math

import jax
import jax.numpy as jnp
import numpy as np
from jax import lax
from jax.experimental import pallas as pl
from jax.experimental.pallas import tpu as pltpu

D_MODEL = 1024
DEPTH = 4
GRID_W = 64
CTX_LEN = 256
HEAD_DIM = 64
NA_HEADS = 8
NA_WIN_H = 8
NA_WIN_W = 16
GQA_Q_HEADS = 8
GQA_KV_HEADS = 2
GQA_REP = GQA_Q_HEADS // GQA_KV_HEADS
NA_WIDTH = NA_HEADS * HEAD_DIM
GQA_Q_WIDTH = GQA_Q_HEADS * HEAD_DIM
GQA_KV_WIDTH = GQA_KV_HEADS * HEAD_DIM
D_FF = -(-(8 * D_MODEL) // (3 * 256)) * 256
ROPE_THETA = 10000.0
EPS = 1e-6
SCALE = HEAD_DIM ** -0.5
LOG2E = math.log2(math.e)
QK_SCALE = SCALE * LOG2E

TOK_TILE = 256
NA_ROWS_PER_TILE = TOK_TILE // GRID_W
NA_KEYS_WIN = NA_WIN_H * GRID_W
N_DR = 2 * NA_WIN_H - 1
N_DC = 2 * NA_WIN_W - 1
FF_CHUNK = 256
NA_PAIRS_PER_STEP = 4
K_EXT = 2 * HEAD_DIM
GQA_MAX_TILE_GROUP = 11
BOUND_SLACK = 1.01
MAX_SHIFT = 60.0
LANES = 128
F32_SUBLANES = 8
STAT_ROWS = 16
BF16_SUBLANES = 16
V_ROWS = HEAD_DIM + BF16_SUBLANES
VMEM_LIMIT_BYTES = 56 * 1024 * 1024

BF16 = jnp.bfloat16
F32 = jnp.float32


def _cparams(n_grid_dims):
    return pltpu.CompilerParams(
        dimension_semantics=("arbitrary",) * n_grid_dims,
        vmem_limit_bytes=VMEM_LIMIT_BYTES)


def _nt_dot(a, b):
    return lax.dot_general(a, b, (((1,), (1,)), ((), ())), preferred_element_type=F32)


def _dot(a, b):
    return jnp.dot(a, b, preferred_element_type=F32)


def _rms_mod(x, g, shift, scale):
    ms = jnp.mean(x * x, axis=-1, keepdims=True)
    y = x * lax.rsqrt(ms + EPS) * g
    return y * (1.0 + scale) + shift


def _mod_kernel(cc_ref, w_ref, b_ref, o_ref):
    cc = cc_ref[...]
    a = (cc * jax.nn.sigmoid(cc)).astype(BF16)
    o_ref[...] = _dot(a, w_ref[...].astype(BF16)) + b_ref[...]


def _mod_call(cc, w_mod, b_mod):
    n_rows = cc.shape[0]
    return pl.pallas_call(
        _mod_kernel,
        grid=(DEPTH, 6),
        in_specs=[
            pl.BlockSpec((n_rows, D_MODEL), lambda l, j: (0, 0)),
            pl.BlockSpec((None, D_MODEL, D_MODEL), lambda l, j: (l, 0, j)),
            pl.BlockSpec((None, 1, D_MODEL), lambda l, j: (l, 0, j)),
        ],
        out_specs=pl.BlockSpec((None, n_rows, D_MODEL), lambda l, j: (l, 0, j)),
        out_shape=jax.ShapeDtypeStruct((DEPTH, n_rows, 6 * D_MODEL), F32),
        compiler_params=_cparams(2),
        name="adaln_mod",
    )(cc, w_mod, b_mod.reshape(DEPTH, 1, 6 * D_MODEL))


def _bias_kernel(rpb_ref, o_ref):
    lh = pl.program_id(0)
    base = lh * (N_DR * N_DC)
    wq = lax.broadcasted_iota(jnp.int32, (GRID_W, 2 * GRID_W), 0)
    lane = lax.broadcasted_iota(jnp.int32, (GRID_W, 2 * GRID_W), 1)
    upper = lane >= GRID_W
    wk = jnp.bitwise_and(lane, GRID_W - 1)
    dci = jnp.clip(wk - wq, -(NA_WIN_W - 1), NA_WIN_W - 1) + (NA_WIN_W - 1)
    c_start = jnp.clip(wq - NA_WIN_W // 2, 0, GRID_W - NA_WIN_W)
    in_win = jnp.logical_and(wk >= c_start, wk < c_start + NA_WIN_W)

    def table(d):
        acc = jnp.zeros((GRID_W, 2 * GRID_W), F32)
        for dc in range(N_DC):
            acc = jnp.where(dci == dc, rpb_ref[base + d * N_DC + dc], acc)
        return jnp.where(in_win, acc * LOG2E, -jnp.inf)

    def body(d, t_d):
        t_next = table(d + 1)
        o_ref[d] = jnp.where(upper, t_next, t_d)
        return t_next

    lax.fori_loop(0, N_DR - 1, body, table(0))


def _bias_call(na_rpb):
    rpb_flat = na_rpb.astype(F32).reshape(-1)
    return pl.pallas_call(
        _bias_kernel,
        grid=(DEPTH * NA_HEADS,),
        in_specs=[pl.BlockSpec(memory_space=pltpu.SMEM)],
        out_specs=pl.BlockSpec((None, N_DR - 1, GRID_W, 2 * GRID_W), lambda i: (i, 0, 0, 0)),
        out_shape=jax.ShapeDtypeStruct((DEPTH * NA_HEADS, N_DR - 1, GRID_W, 2 * GRID_W), F32),
        compiler_params=_cparams(1),
        name="na_bias_slabs",
    )(rpb_flat)


def _token_tile(lat_ref, ctx_ref, n_lat_tiles, combined):
    if combined:
        return lat_ref[...]
    return jnp.where(pl.program_id(1) < n_lat_tiles, lat_ref[...], ctx_ref[...])


def _token_specs(n_lat_tiles, combined):
    if combined:
        lat_map = lambda bi, i: (bi, i, 0)
        ctx_map = lambda bi, i: (bi, n_lat_tiles, 0)
    else:
        lat_map = lambda bi, i: (bi, jnp.minimum(i, n_lat_tiles - 1), 0)
        ctx_map = lambda bi, i: (bi, 0, 0)
    return [pl.BlockSpec((None, TOK_TILE, D_MODEL), lat_map),
            pl.BlockSpec((None, TOK_TILE, D_MODEL), ctx_map)]


def _inproj_kernel(lat_ref, ctx_ref, mod_ref, n1_ref, wna_ref, wgt_ref, qg_ref, kg_ref, cos_ref,
                   sin_ref, na_ref, gq_ref, gk_ref, gv_ref, st_ref, *, n_lat_tiles):
    _inproj_body(_token_tile(lat_ref, ctx_ref, n_lat_tiles, False), mod_ref, n1_ref, wna_ref, wgt_ref,
                 qg_ref, kg_ref, cos_ref, sin_ref, na_ref, gq_ref, gk_ref, gv_ref, st_ref)


def _inproj_body(x, mod_ref, n1_ref, wna_ref, wgt_ref, qg_ref, kg_ref, cos_ref, sin_ref,
                 na_ref, gq_ref, gk_ref, gv_ref, st_ref):
    h = _rms_mod(x, n1_ref[...], mod_ref[0:1, :], mod_ref[1:2, :]).astype(BF16)

    na = _dot(h, wna_ref[...])
    na_ref[:, 0:NA_WIDTH] = (na[:, 0:NA_WIDTH] * QK_SCALE).astype(BF16)
    na_ref[:, NA_WIDTH:3 * NA_WIDTH] = na[:, NA_WIDTH:3 * NA_WIDTH].astype(BF16)

    gt = _nt_dot(wgt_ref[...], h)
    cos = cos_ref[...]
    sin = sin_ref[...]
    half = HEAD_DIM // 2

    def norm_rope(xh, gain):
        ms = jnp.sum(xh * xh, axis=0, keepdims=True) * (1.0 / HEAD_DIM)
        y = xh * lax.rsqrt(ms + EPS) * gain
        y1 = y[0:half]
        y2 = y[half:HEAD_DIM]
        return jnp.concatenate([y1 * cos - y2 * sin, y1 * sin + y2 * cos], axis=0)

    def max_sq_norm(y):
        n2 = jnp.sum(y * y, axis=0, keepdims=True)
        return jnp.broadcast_to(jnp.max(n2, axis=1, keepdims=True), (1, LANES))

    st_ref[...] = jnp.zeros(st_ref.shape, F32)
    qg = qg_ref[...]
    for hd in range(GQA_Q_HEADS):
        q = norm_rope(gt[hd * HEAD_DIM:(hd + 1) * HEAD_DIM], qg) * QK_SCALE
        gq_ref[hd] = q.astype(BF16)
        st_ref[hd:hd + 1, :] = max_sq_norm(q)
    kg = kg_ref[...]
    ones_col = (lax.broadcasted_iota(jnp.int32, (K_EXT - HEAD_DIM, TOK_TILE), 0) == 0).astype(F32)
    for kv in range(GQA_KV_HEADS):
        r0 = GQA_Q_WIDTH + kv * HEAD_DIM
        k = norm_rope(gt[r0:r0 + HEAD_DIM], kg)
        st_ref[GQA_Q_HEADS + kv:GQA_Q_HEADS + kv + 1, :] = max_sq_norm(k)
        gk_ref[kv] = jnp.concatenate([k, ones_col], axis=0).T.astype(BF16)
        r1 = GQA_Q_WIDTH + GQA_KV_WIDTH + kv * HEAD_DIM
        gv_ref[kv, 0:HEAD_DIM] = gt[r1:r1 + HEAD_DIM].astype(BF16)
        gv_ref[kv, HEAD_DIM:V_ROWS] = jnp.ones((V_ROWS - HEAD_DIM, TOK_TILE), BF16)


def _const_spec(shp):
    return pl.BlockSpec(shp, lambda bi, i: (0,) * len(shp), pipeline_mode=pl.Buffered(1))


def _mod_spec(n_lat_tiles):
    return pl.BlockSpec((None, None, 6, D_MODEL), lambda bi, i: (bi, i // n_lat_tiles, 0, 0))


def _inproj_param_specs(n_lat_tiles):
    return [
        _mod_spec(n_lat_tiles),
        _const_spec((1, D_MODEL)),
        _const_spec((D_MODEL, 3 * NA_WIDTH)),
        _const_spec((GQA_Q_WIDTH + 2 * GQA_KV_WIDTH, D_MODEL)),
        _const_spec((HEAD_DIM, 1)),
        _const_spec((HEAD_DIM, 1)),
        pl.BlockSpec((HEAD_DIM // 2, TOK_TILE), lambda bi, i: (0, i)),
        pl.BlockSpec((HEAD_DIM // 2, TOK_TILE), lambda bi, i: (0, i)),
    ]


def _inproj_out_specs(b, nt):
    n_tok = nt * TOK_TILE
    specs = [
        pl.BlockSpec((None, TOK_TILE, 3 * NA_WIDTH), lambda bi, i: (bi, i, 0)),
        pl.BlockSpec((None, GQA_Q_HEADS, HEAD_DIM, TOK_TILE), lambda bi, i: (bi, 0, 0, i)),
        pl.BlockSpec((None, GQA_KV_HEADS, TOK_TILE, K_EXT), lambda bi, i: (bi, 0, i, 0)),
        pl.BlockSpec((None, GQA_KV_HEADS, None, V_ROWS, TOK_TILE), lambda bi, i: (bi, 0, i, 0, 0)),
        pl.BlockSpec((None, None, STAT_ROWS, LANES), lambda bi, i: (bi, i, 0, 0)),
    ]
    shapes = [
        jax.ShapeDtypeStruct((b, n_tok, 3 * NA_WIDTH), BF16),
        jax.ShapeDtypeStruct((b, GQA_Q_HEADS, HEAD_DIM, n_tok), BF16),
        jax.ShapeDtypeStruct((b, GQA_KV_HEADS, n_tok, K_EXT), BF16),
        jax.ShapeDtypeStruct((b, GQA_KV_HEADS, nt, V_ROWS, TOK_TILE), BF16),
        jax.ShapeDtypeStruct((b, nt, STAT_ROWS, LANES), F32),
    ]
    return specs, shapes


def _inproj_call(lat, ctx_src, params, n_lat_tiles):
    b = lat.shape[0]
    nt = n_lat_tiles + 1
    out_specs, out_shape = _inproj_out_specs(b, nt)
    return pl.pallas_call(
        functools.partial(_inproj_kernel, n_lat_tiles=n_lat_tiles),
        grid=(b, nt),
        in_specs=_token_specs(n_lat_tiles, False) + _inproj_param_specs(n_lat_tiles),
        out_specs=out_specs,
        out_shape=out_shape,
        compiler_params=_cparams(2),
        name="in_proj",
    )(lat, ctx_src, *params)


def _na_kernel(q_ref, k_ref, v_ref, w_ref, o_ref, *, n_lat_tiles, n_rows):
    i = pl.program_id(2)
    n_lat = n_lat_tiles * TOK_TILE
    pair_w = 2 * HEAD_DIM
    lane = lax.broadcasted_iota(jnp.int32, (1, pair_w), 1)
    lo_half = lane < HEAD_DIM
    pairs = range(NA_PAIRS_PER_STEP)
    cols = [slice(pp * pair_w, (pp + 1) * pair_w) for pp in pairs]

    def with_ones(v):
        return jnp.concatenate([v, jnp.ones(v.shape, BF16)], axis=1)

    def both_heads(x, rr):
        lo = rr * GRID_W
        return jnp.concatenate([x[lo:lo + GRID_W], x[TOK_TILE + lo:TOK_TILE + lo + GRID_W]], axis=0)

    def normalise(o):
        return o[:, 0:pair_w] / o[:, pair_w:2 * pair_w]

    q2s = []
    s_ctxs = []
    v_ctxs = []
    for pp in pairs:
        q = q_ref[:, cols[pp]]
        zero = jnp.zeros_like(q)
        q2 = jnp.concatenate([jnp.where(lo_half, q, zero), jnp.where(lo_half, zero, q)], axis=0)
        q2s.append(q2)
        s_ctxs.append(_nt_dot(q2, k_ref[n_lat:n_lat + CTX_LEN, cols[pp]]))
        v_ctxs.append(with_ones(v_ref[n_lat:n_lat + CTX_LEN, cols[pp]]))

    @pl.when(i < n_lat_tiles)
    def _latent():
        units = [(pp, rr) for pp in pairs for rr in range(NA_ROWS_PER_TILE)]
        k0s = []
        d0s = []
        for rr in range(NA_ROWS_PER_TILE):
            r = i * NA_ROWS_PER_TILE + rr
            r_start = jnp.clip(r - NA_WIN_H // 2, 0, n_rows - NA_WIN_H)
            d0s.append(r_start - r + (NA_WIN_H - 1))
            k0s.append(pl.multiple_of(r_start * GRID_W, GRID_W))
        s_wins = {}
        for pp, rr in units:
            k_win = k_ref[pl.ds(k0s[rr], NA_KEYS_WIN), cols[pp]]
            bias = jnp.concatenate(
                [jnp.concatenate([w_ref[2 * pp + hh, d0s[rr] + 2 * jj]
                                  for jj in range(NA_WIN_H // 2)], axis=1)
                 for hh in range(2)], axis=0)
            s_wins[pp, rr] = _nt_dot(both_heads(q2s[pp], rr), k_win) + bias
        p_wins = {}
        p_ctxs = {}
        for pp, rr in units:
            s_w = s_wins[pp, rr]
            s_c = both_heads(s_ctxs[pp], rr)
            m = jnp.maximum(jnp.max(s_w, axis=-1, keepdims=True),
                            jnp.max(s_c, axis=-1, keepdims=True))
            p_wins[pp, rr] = jnp.exp2(s_w - m).astype(BF16)
            p_ctxs[pp, rr] = jnp.exp2(s_c - m).astype(BF16)
        for pp in pairs:
            rows = range(NA_ROWS_PER_TILE)
            p_ctx = jnp.concatenate([p_ctxs[pp, rr][0:GRID_W] for rr in rows]
                                    + [p_ctxs[pp, rr][GRID_W:] for rr in rows], axis=0)
            o_ctx = _dot(p_ctx, v_ctxs[pp])
            for rr in rows:
                v_win = with_ones(v_ref[pl.ds(k0s[rr], NA_KEYS_WIN), cols[pp]])
                o = normalise(_dot(p_wins[pp, rr], v_win) + both_heads(o_ctx, rr))
                o_ref[rr * GRID_W:(rr + 1) * GRID_W, cols[pp]] = jnp.where(
                    lo_half, o[0:GRID_W], o[GRID_W:]).astype(BF16)

    @pl.when(i >= n_lat_tiles)
    def _context():
        for pp in pairs:
            s = s_ctxs[pp]
            p = jnp.exp2(s - jnp.max(s, axis=-1, keepdims=True))
            o = normalise(_dot(p.astype(BF16), v_ctxs[pp]))
            o_ref[:, cols[pp]] = jnp.where(lo_half, o[0:TOK_TILE], o[TOK_TILE:]).astype(BF16)


def _na_call(na_qkv, bias_slabs, n_q_tiles, n_lat_tiles):
    b, n_tok, _ = na_qkv.shape
    n_groups = NA_HEADS // (2 * NA_PAIRS_PER_STEP)
    width = 2 * HEAD_DIM * NA_PAIRS_PER_STEP
    kern = functools.partial(_na_kernel, n_lat_tiles=n_lat_tiles,
                             n_rows=n_lat_tiles * NA_ROWS_PER_TILE)
    return pl.pallas_call(
        kern,
        grid=(b, n_groups, n_q_tiles),
        in_specs=[
            pl.BlockSpec((None, TOK_TILE, width), lambda bi, hg, i: (bi, i, hg)),
            pl.BlockSpec((None, n_tok, width), lambda bi, hg, i: (bi, 0, n_groups + hg),
                         pipeline_mode=pl.Buffered(1)),
            pl.BlockSpec((None, n_tok, width), lambda bi, hg, i: (bi, 0, 2 * n_groups + hg),
                         pipeline_mode=pl.Buffered(1)),
            pl.BlockSpec((2 * NA_PAIRS_PER_STEP, N_DR - 1, GRID_W, 2 * GRID_W),
                         lambda bi, hg, i: (hg, 0, 0, 0)),
        ],
        out_specs=pl.BlockSpec((None, TOK_TILE, width), lambda bi, hg, i: (bi, i, hg)),
        out_shape=jax.ShapeDtypeStruct((b, n_q_tiles * TOK_TILE, NA_WIDTH), BF16),
        compiler_params=_cparams(3),
        name="na_attn",
    )(na_qkv, na_qkv, na_qkv, bias_slabs)


def _gqa_finish(o_ref, accs):
    outs = [a[0:HEAD_DIM] / a[HEAD_DIM:HEAD_DIM + 1] for a in accs]
    o_ref[...] = jnp.concatenate(outs, axis=0).T.astype(BF16)


def _gqa_bounded_kernel(q_ref, k_ref, v_ref, sh_ref, o_ref, qx_sc, p_sc, l_sc, acc_sc, *,
                        n_lat_tiles, group):
    i = pl.program_id(2)
    n_key_tiles = n_lat_tiles + 1
    row = lax.broadcasted_iota(jnp.int32, (K_EXT - HEAD_DIM, TOK_TILE), 0)
    for hd in range(GQA_REP):
        qx_sc[hd, 0:HEAD_DIM] = q_ref[hd]
        qx_sc[hd, HEAD_DIM:K_EXT] = jnp.where(row == 0, -sh_ref[hd][:, 0:1], 0.0).astype(BF16)
    l_sc[...] = jnp.zeros(l_sc.shape, F32)

    def key_tile(c):
        return k_ref[pl.ds(pl.multiple_of(c * TOK_TILE, TOK_TILE), TOK_TILE), :]

    def value_tile(c):
        return v_ref[c, 0:HEAD_DIM, :]

    def probs(k_c, hd):
        p = jnp.exp2(_dot(k_c, qx_sc[hd]))
        l_sc[hd] = l_sc[hd] + jnp.sum(p.reshape(TOK_TILE // F32_SUBLANES, F32_SUBLANES, TOK_TILE),
                                      axis=0)
        return p.astype(BF16)

    def finish(accs):
        outs = [accs[hd] / jnp.sum(l_sc[hd], axis=0, keepdims=True) for hd in range(GQA_REP)]
        o_ref[...] = jnp.concatenate(outs, axis=0).T.astype(BF16)

    @pl.when(i < n_lat_tiles)
    def _latent():
        acc_sc[...] = jnp.zeros(acc_sc.shape, F32)
        p_sc[...] = jnp.zeros(p_sc.shape, BF16)

        def tiles(gi, carry):
            c0 = gi * group
            p_prev = [p_sc[hd] for hd in range(GQA_REP)]
            c_prev = jnp.maximum(c0 - 1, 0)
            for cc in range(group):
                k_c = key_tile(c0 + cc)
                v_p = value_tile(c_prev)
                p_new = []
                for hd in range(GQA_REP):
                    p_new.append(probs(k_c, hd))
                    acc_sc[hd] = acc_sc[hd] + _dot(v_p, p_prev[hd])
                p_prev = p_new
                c_prev = c0 + cc
            for hd in range(GQA_REP):
                p_sc[hd] = p_prev[hd]
            return carry

        lax.fori_loop(0, n_key_tiles // group, tiles, 0)
        v_p = value_tile(n_key_tiles - 1)
        finish([acc_sc[hd] + _dot(v_p, p_sc[hd]) for hd in range(GQA_REP)])

    @pl.when(i >= n_lat_tiles)
    def _context():
        k_c = key_tile(n_lat_tiles)
        v_c = value_tile(n_lat_tiles)
        finish([_dot(v_c, probs(k_c, hd)) for hd in range(GQA_REP)])


def _gqa_kernel(q_ref, k_ref, v_ref, o_ref, s_a, s_b, cm_a, cm_b, p_a, p_b, m_sc, acc_sc, *,
                n_lat_tiles):
    i = pl.program_id(2)

    def key_tile(c):
        return k_ref[pl.ds(pl.multiple_of(c * TOK_TILE, TOK_TILE), TOK_TILE), 0:HEAD_DIM]

    def finish(accs):
        _gqa_finish(o_ref, accs)

    def step(c_prev, c_next, s_cur, cm_cur, s_nxt, cm_nxt, p_prev, p_cur):
        v_prev = v_ref[c_prev]
        k_next = key_tile(c_next)
        for hd in range(GQA_REP):
            pv = _dot(v_prev, p_prev[hd])
            s = _dot(k_next, q_ref[hd])
            s_nxt[hd] = s
            cm_nxt[hd] = jnp.max(s, axis=0, keepdims=True)
            m_old = m_sc[hd]
            m_new = jnp.maximum(m_old, cm_cur[hd])
            alpha = jnp.exp2(m_old - m_new)
            p_cur[hd] = jnp.exp2(s_cur[hd] - m_new).astype(BF16)
            acc_sc[hd] = (acc_sc[hd] + pv) * alpha
            m_sc[hd] = m_new

    @pl.when(i < n_lat_tiles)
    def _latent():
        m_sc[...] = jnp.full(m_sc.shape, -jnp.inf, F32)
        acc_sc[...] = jnp.zeros(acc_sc.shape, F32)
        p_b[...] = jnp.zeros(p_b.shape, BF16)
        k_0 = key_tile(0)
        for hd in range(GQA_REP):
            s = _dot(k_0, q_ref[hd])
            s_a[hd] = s
            cm_a[hd] = jnp.max(s, axis=0, keepdims=True)

        def pair(j, carry):
            c = 2 * j
            step(jnp.maximum(c - 1, 0), c + 1, s_a, cm_a, s_b, cm_b, p_b, p_a)
            step(c, c + 2, s_b, cm_b, s_a, cm_a, p_a, p_b)
            return carry

        lax.fori_loop(0, n_lat_tiles // 2, pair, 0, unroll=8)
        v_last = v_ref[n_lat_tiles - 1]
        v_ctx = v_ref[n_lat_tiles]
        accs = []
        for hd in range(GQA_REP):
            pv = _dot(v_last, p_b[hd])
            m_old = m_sc[hd]
            m_new = jnp.maximum(m_old, cm_a[hd])
            alpha = jnp.exp2(m_old - m_new)
            p = jnp.exp2(s_a[hd] - m_new).astype(BF16)
            accs.append((acc_sc[hd] + pv) * alpha + _dot(v_ctx, p))
        finish(accs)

    @pl.when(i >= n_lat_tiles)
    def _context():
        k_c = key_tile(n_lat_tiles)
        v_c = v_ref[n_lat_tiles]
        accs = []
        for hd in range(GQA_REP):
            s = _dot(k_c, q_ref[hd])
            p = jnp.exp2(s - jnp.max(s, axis=0, keepdims=True)).astype(BF16)
            accs.append(_dot(v_c, p))
        finish(accs)


def _gqa_specs(gq, gv, n_q_tiles):
    b = gq.shape[0]
    n_tok = gq.shape[-1]
    n_key_tiles = gv.shape[2]
    in_specs = [
        pl.BlockSpec((None, GQA_REP, HEAD_DIM, TOK_TILE), lambda bi, g, i: (bi, g, 0, i)),
        pl.BlockSpec((None, None, n_tok, K_EXT), lambda bi, g, i: (bi, g, 0, 0)),
        pl.BlockSpec((None, None, n_key_tiles, V_ROWS, TOK_TILE), lambda bi, g, i: (bi, g, 0, 0, 0)),
    ]
    out_spec = pl.BlockSpec((None, TOK_TILE, GQA_REP * HEAD_DIM), lambda bi, g, i: (bi, i, g))
    out_shape = jax.ShapeDtypeStruct((b, n_q_tiles * TOK_TILE, GQA_Q_WIDTH), BF16)
    return (b, GQA_KV_HEADS, n_q_tiles), in_specs, out_spec, out_shape


def _gqa_online_call(gq, gk, gv, n_q_tiles, n_lat_tiles):
    assert n_lat_tiles % 2 == 0 and gv.shape[2] == n_lat_tiles + 1
    grid, in_specs, out_spec, out_shape = _gqa_specs(gq, gv, n_q_tiles)
    score_buf = pltpu.VMEM((GQA_REP, TOK_TILE, TOK_TILE), F32)
    stat_buf = pltpu.VMEM((GQA_REP, 1, TOK_TILE), F32)
    prob_buf = pltpu.VMEM((GQA_REP, TOK_TILE, TOK_TILE), BF16)
    return pl.pallas_call(
        functools.partial(_gqa_kernel, n_lat_tiles=n_lat_tiles),
        grid=grid, in_specs=in_specs, out_specs=out_spec, out_shape=out_shape,
        scratch_shapes=[score_buf, score_buf, stat_buf, stat_buf, prob_buf, prob_buf, stat_buf,
                        pltpu.VMEM((GQA_REP, V_ROWS, TOK_TILE), F32)],
        compiler_params=_cparams(3),
        name="gqa_attn_online",
    )(gq, gk, gv)


def _gqa_bounded_call(gq, gk, gv, shift, n_q_tiles, n_lat_tiles):
    n_key_tiles = gv.shape[2]
    assert n_key_tiles == n_lat_tiles + 1
    group = max(g for g in range(1, GQA_MAX_TILE_GROUP + 1) if n_key_tiles % g == 0)
    grid, in_specs, out_spec, out_shape = _gqa_specs(gq, gv, n_q_tiles)
    in_specs.append(pl.BlockSpec((None, GQA_REP, 1, LANES), lambda bi, g, i: (bi, g, 0, 0)))
    return pl.pallas_call(
        functools.partial(_gqa_bounded_kernel, n_lat_tiles=n_lat_tiles, group=group),
        grid=grid, in_specs=in_specs, out_specs=out_spec, out_shape=out_shape,
        scratch_shapes=[pltpu.VMEM((GQA_REP, K_EXT, TOK_TILE), BF16),
                        pltpu.VMEM((GQA_REP, TOK_TILE, TOK_TILE), BF16),
                        pltpu.VMEM((GQA_REP, F32_SUBLANES, TOK_TILE), F32),
                        pltpu.VMEM((GQA_REP, HEAD_DIM, TOK_TILE), F32)],
        compiler_params=_cparams(3),
        name="gqa_attn_bounded",
    )(gq, gk, gv, shift)


def _gqa_call(gq, gk, gv, stats, n_q_tiles, n_lat_tiles):
    n2 = jnp.max(stats[:, :, :, 0], axis=1)
    q_max = jnp.sqrt(n2[:, 0:GQA_Q_HEADS])
    k_max = jnp.repeat(jnp.sqrt(n2[:, GQA_Q_HEADS:GQA_Q_HEADS + GQA_KV_HEADS]), GQA_REP, axis=1)
    bound = q_max * k_max * BOUND_SLACK
    shift = jnp.broadcast_to(bound[:, :, None, None], bound.shape + (1, LANES))
    return lax.cond(
        jnp.max(bound) <= MAX_SHIFT,
        lambda: _gqa_bounded_call(gq, gk, gv, shift, n_q_tiles, n_lat_tiles),
        lambda: _gqa_online_call(gq, gk, gv, n_q_tiles, n_lat_tiles))


def _mix_ffn_kernel(lat_ref, ctx_ref, mod_ref, n1_ref, n2_ref, ya_ref, yb_ref, wg_ref, wpa_ref,
                    wpb_ref, wo_ref, win_ref, wout_ref, fn_ref, *rest, final_norm, n_lat_tiles,
                    combined):
    n_par = len(rest) - 2 if final_norm else (len(rest) - 2 - 5)
    next_params = rest[0:n_par]
    o_ref = rest[n_par]
    next_outs = rest[n_par + 1:-1]
    g_sc = rest[-1]
    x = _token_tile(lat_ref, ctx_ref, n_lat_tiles, combined)
    h = _rms_mod(x, n1_ref[...], mod_ref[0:1, :], mod_ref[1:2, :]).astype(BF16)
    gates = _dot(h, wg_ref[...])
    pa = _dot(ya_ref[...], wpa_ref[...])
    pb = _dot(yb_ref[...], wpb_ref[...])
    merged = (jax.nn.sigmoid(gates[:, 0:D_MODEL]) * pa
              + jax.nn.sigmoid(gates[:, D_MODEL:2 * D_MODEL]) * pb)
    x = x + mod_ref[2:3, :] * _dot(merged.astype(BF16), wo_ref[...])
    h = _rms_mod(x, n2_ref[...], mod_ref[3:4, :], mod_ref[4:5, :]).astype(BF16)
    for j in range(D_FF // FF_CHUNK):
        c0 = j * FF_CHUNK
        a = _dot(h, win_ref[:, c0:c0 + FF_CHUNK])
        u = _dot(h, win_ref[:, D_FF + c0:D_FF + c0 + FF_CHUNK])
        g_sc[:, c0:c0 + FF_CHUNK] = (a * jax.nn.sigmoid(a) * u).astype(BF16)
    y = x + mod_ref[5:6, :] * _dot(g_sc[...], wout_ref[...])
    if final_norm:
        ms = jnp.mean(y * y, axis=-1, keepdims=True)
        y = y * lax.rsqrt(ms + EPS) * fn_ref[...]
    o_ref[...] = y
    if not final_norm:
        _inproj_body(y, *next_params, *next_outs)


def _mix_ffn_call(lat, ctx_src, mod, n1, n2, ya, yb, wg, wpa, wpb, wo, win, wout, fn, next_params,
                  n_tiles, n_lat_tiles):
    b = lat.shape[0]
    final_norm = next_params is None
    combined = lat is ctx_src
    const = _const_spec
    tile = lambda width: pl.BlockSpec((None, TOK_TILE, width), lambda bi, i: (bi, i, 0))
    in_specs = _token_specs(n_lat_tiles, combined) + [
        _mod_spec(n_lat_tiles), const((1, D_MODEL)), const((1, D_MODEL)),
        tile(NA_WIDTH), tile(GQA_Q_WIDTH),
        const((D_MODEL, 2 * D_MODEL)), const((NA_WIDTH, D_MODEL)), const((GQA_Q_WIDTH, D_MODEL)),
        const((D_MODEL, D_MODEL)), const((D_MODEL, 2 * D_FF)), const((D_FF, D_MODEL)),
        const((1, D_MODEL)),
    ]
    out_specs = [tile(D_MODEL)]
    out_shape = [jax.ShapeDtypeStruct((b, n_tiles * TOK_TILE, D_MODEL), F32)]
    operands = [lat, ctx_src, mod, n1, n2, ya, yb, wg, wpa, wpb, wo, win, wout, fn]
    if not final_norm:
        in_specs += _inproj_param_specs(n_lat_tiles)
        operands += list(next_params)
        specs, shapes = _inproj_out_specs(b, n_tiles)
        out_specs += specs
        out_shape += shapes
    return pl.pallas_call(
        functools.partial(_mix_ffn_kernel, final_norm=final_norm, n_lat_tiles=n_lat_tiles,
                          combined=combined),
        grid=(b, n_tiles),
        in_specs=in_specs,
        out_specs=out_specs,
        out_shape=out_shape,
        scratch_shapes=[pltpu.VMEM((TOK_TILE, D_FF), BF16)],
        compiler_params=_cparams(2),
        name="mix_ffn",
    )(*operands)


def _rope_tables(n_lat, n_ctx):
    t = jnp.arange(n_lat)
    row = (t // GRID_W).astype(F32)
    col = (t % GRID_W).astype(F32)
    half = HEAD_DIM // 2
    inv = ROPE_THETA ** (-jnp.arange(0, half, 2, dtype=F32) / half)
    ang = jnp.concatenate([row[:, None] * inv, col[:, None] * inv], axis=-1)
    cos_t = jnp.concatenate([jnp.cos(ang).T, jnp.ones((half, n_ctx), F32)], axis=1)
    sin_t = jnp.concatenate([jnp.sin(ang).T, jnp.zeros((half, n_ctx), F32)], axis=1)
    return cos_t, sin_t


def kernel(x, c, ctx, c_ctx, w_mod, b_mod, norm1, w_in, na_rpb, q_gain, k_gain, w_pa, w_pb, w_o,
           norm2, w_ffn_in, w_ffn_out, final_norm):
    b, s, d = x.shape
    n_ctx = ctx.shape[1]
    assert d == D_MODEL and n_ctx == CTX_LEN == TOK_TILE and s % TOK_TILE == 0
    assert s // GRID_W >= NA_WIN_H
    n_lat_tiles = s // TOK_TILE
    n_tiles = n_lat_tiles + 1

    perm = np.concatenate([np.arange(0, HEAD_DIM, 2), np.arange(1, HEAD_DIM, 2)])
    q_cols = (np.arange(GQA_Q_HEADS)[:, None] * HEAD_DIM + perm[None, :]).reshape(-1)
    k_cols = (np.arange(GQA_KV_HEADS)[:, None] * HEAD_DIM + perm[None, :]).reshape(-1)
    o_gq = 3 * NA_WIDTH
    o_gk = o_gq + GQA_Q_WIDTH
    o_gv = o_gk + GQA_KV_WIDTH
    o_ga = o_gv + GQA_KV_WIDTH

    cos_t, sin_t = _rope_tables(s, n_ctx)

    assert b + 1 <= F32_SUBLANES
    cc = jnp.zeros((F32_SUBLANES, D_MODEL), F32).at[0:b].set(c).at[b].set(c_ctx)
    mod_all = _mod_call(cc, w_mod, b_mod)
    mod_lat = mod_all[:, 0:b].reshape(DEPTH, b, 1, 6, D_MODEL)
    mod_ctx = jnp.broadcast_to(mod_all[:, b].reshape(DEPTH, 1, 1, 6, D_MODEL),
                               (DEPTH, b, 1, 6, D_MODEL))
    mod_sel = jnp.concatenate([mod_lat, mod_ctx], axis=2)

    bias_slabs = _bias_call(na_rpb).reshape(DEPTH, NA_HEADS, N_DR - 1, GRID_W, 2 * GRID_W)

    def inproj_params(l):
        w_l = w_in[l]
        wna = w_l[:, 0:o_gq].astype(BF16)
        wgt = jnp.concatenate(
            [w_l[:, o_gq:o_gk][:, q_cols], w_l[:, o_gk:o_gv][:, k_cols], w_l[:, o_gv:o_ga]],
            axis=1).T.astype(BF16)
        return (mod_sel[l], norm1[l].reshape(1, D_MODEL), wna, wgt,
                q_gain[l][perm].reshape(HEAD_DIM, 1), k_gain[l][perm].reshape(HEAD_DIM, 1),
                cos_t, sin_t)

    lat, ctx_src = x, ctx
    mixer_in = _inproj_call(lat, ctx_src, inproj_params(0), n_lat_tiles)
    for l in range(DEPTH):
        last = l == DEPTH - 1
        na_qkv, gq, gk, gv, stats = mixer_in
        n_q_tiles = n_lat_tiles if last else n_tiles
        ya = _na_call(na_qkv, bias_slabs[l], n_q_tiles, n_lat_tiles)
        yb = _gqa_call(gq, gk, gv, stats, n_q_tiles, n_lat_tiles)
        outs = _mix_ffn_call(lat, ctx_src, mod_sel[l], norm1[l].reshape(1, D_MODEL),
                             norm2[l].reshape(1, D_MODEL), ya, yb, w_in[l][:, o_ga:].astype(BF16),
                             w_pa[l].astype(BF16), w_pb[l].astype(BF16), w_o[l].astype(BF16),
                             w_ffn_in[l].astype(BF16), w_ffn_out[l].astype(BF16),
                             final_norm.reshape(1, D_MODEL),
                             None if last else inproj_params(l + 1), n_q_tiles, n_lat_tiles)
        lat = ctx_src = outs[0]
        mixer_in = outs[1:]
    return lat
```
